```python
import jax, jax.numpy as jnp
from jax import lax
import numpy as np

D_MODEL = 1024
BATCH = 8
SEQ = 4096
DEPTH = 2
DEC_BATCH = 32
DEC_SEQ = 1
PAST_LEN = 16384
PAGE_SIZE = 128

N_HEADS = 16
HEAD_DIM = D_MODEL // N_HEADS
ATT_W = N_HEADS * HEAD_DIM
EXPAND = 2
D_INNER = EXPAND * D_MODEL
CHUNK = 128
N_GROUPS = 8
GROUP_DIM = D_INNER // N_GROUPS
Q_BLOCK = 128
N_A = DEPTH // 2
N_B = DEPTH - N_A
ALPHA = (2 * DEPTH) ** 0.25
BETA = (8 * DEPTH) ** -0.25
LN_EPS = 1e-5
ATTN_SCALE = HEAD_DIM ** -0.5
SB_BIAS_INIT = -8.0

kernel_name = 'yoco_gmlp_stickbreaking_step'


def layer_norm(x, g, b):
    xf = x.astype(jnp.float32)
    mu = jnp.mean(xf, axis=-1, keepdims=True)
    var = jnp.mean(jnp.square(xf - mu), axis=-1, keepdims=True)
    return ((xf - mu) * lax.rsqrt(var + LN_EPS) * g + b).astype(x.dtype)


def ada_terms(c, w, b, n):
    ada = jax.nn.silu(c) @ w + b
    return [a[:, None, :] for a in jnp.split(ada, n, axis=-1)]


def chunk_mix(v, w_s, b_s):
    bsz, t, _ = v.shape
    n_chunks = -(-t // CHUNK)
    pad = n_chunks * CHUNK - t
    vc = jnp.pad(v, ((0, 0), (0, pad), (0, 0))).reshape(bsz, n_chunks, CHUNK, N_GROUPS, GROUP_DIM)
    w = w_s * jnp.tril(jnp.ones((CHUNK, CHUNK), w_s.dtype))
    mixed = jnp.einsum('gts,bcsgd->bctgd', w, vc) + b_s.T[:, :, None]
    return mixed.reshape(bsz, n_chunks * CHUNK, D_INNER)[:, :t]


def mixer_a(h, w_in, sgu_g, sgu_b, w_s, b_s, w_out):
    u, v, zg = jnp.split(h @ w_in, 3, axis=-1)
    vn = layer_norm(v, sgu_g, sgu_b)
    y = (u * chunk_mix(vn, w_s, b_s) * jax.nn.silu(zg)) @ w_out
    return y, vn


def stick_breaking(q, k, v, bias, q_pos, k_pos):
    z = (jnp.einsum('bqhd,bkhd->bhqk', q, k).astype(jnp.float32) * ATTN_SCALE
         + bias.astype(jnp.float32)[None, :, None, None])
    mask = k_pos[None, :] < q_pos[:, None]
    log_1m = jnp.where(mask, jax.nn.log_sigmoid(-z), 0.0)
    after = lax.cumsum(log_1m, axis=3, reverse=True) - log_1m
    w = jnp.where(mask, jnp.exp(jax.nn.log_sigmoid(z) + after), 0.0)
    return jnp.einsum('bhqk,bkhd->bqhd', w.astype(v.dtype), v)


def sb_prompt(q, k, v, bias):
    t = q.shape[1]
    pos = jnp.arange(t, dtype=jnp.int32)
    outs = []
    for i in range(t // Q_BLOCK):
        lo, hi = i * Q_BLOCK, (i + 1) * Q_BLOCK
        outs.append(stick_breaking(q[:, lo:hi], k[:, :hi], v[:, :hi], bias, pos[lo:hi], pos[:hi]))
    return jnp.concatenate(outs, axis=1)


def trunk(x, c, attend, w_ada, b_ada, ln_g, ln_b, w_in_a, sgu_g, sgu_b, w_s, b_s, w_out_a,
          w_ada_kv, b_ada_kv, w_kv, w_in_b, w_out_b, sb_bias):
    bsz, t, _ = x.shape
    k = v = None
    chunk_vs = []
    for layer in range(DEPTH):
        shift, scale, gate = ada_terms(c, w_ada[layer], b_ada[layer], 3)
        h = x * (1.0 + scale) + shift
        if layer < N_A:
            y, vn = mixer_a(h, w_in_a[layer], sgu_g[layer], sgu_b[layer], w_s[layer], b_s[layer], w_out_a[layer])
            chunk_vs.append(vn)
        else:
            if layer == N_A:
                kshift, kscale = ada_terms(c, w_ada_kv, b_ada_kv, 2)
                hk = x * (1.0 + kscale) + kshift
                k, v = jnp.split(hk @ w_kv, 2, axis=-1)
                k = k.reshape(bsz, t, N_HEADS, HEAD_DIM)
                v = v.reshape(bsz, t, N_HEADS, HEAD_DIM)
            j = layer - N_A
            q, zg = jnp.split(h @ w_in_b[j], 2, axis=-1)
            o = attend(q.reshape(bsz, t, N_HEADS, HEAD_DIM), k, v, sb_bias[j]).reshape(bsz, t, ATT_W)
            y = (o * jax.nn.silu(zg)) @ w_out_b[j]
        x = layer_norm(ALPHA * x + gate * y, ln_g[layer], ln_b[layer])
    return x, k, v, jnp.stack(chunk_vs, axis=0)


def setup_inputs(seed: int = 0) -> dict:
    key = jax.random.key(seed)
    ks = jax.random.split(key, 26)
    n_pages = PAST_LEN // PAGE_SIZE
    n_used = DEC_BATCH * n_pages
    n_pool = n_used + n_used // 4
    nrm = lambda k, shape, s: jax.random.normal(k, shape, jnp.float32) * s
    page_table = jax.random.permutation(ks[0], n_pool)[:n_used].reshape(DEC_BATCH, n_pages).astype(jnp.int32)
    w_kv = jnp.concatenate([nrm(ks[1], (D_MODEL, ATT_W), D_MODEL ** -0.5),
                            nrm(ks[2], (D_MODEL, ATT_W), BETA * D_MODEL ** -0.5)], axis=1)
    w_in_b = jnp.concatenate([nrm(ks[3], (N_B, D_MODEL, ATT_W), D_MODEL ** -0.5),
                              nrm(ks[4], (N_B, D_MODEL, ATT_W), D_MODEL ** -0.5)], axis=2)
    return {
        'x_prompt': nrm(ks[5], (BATCH, SEQ, D_MODEL), 1.0),
        'x_sample': nrm(ks[6], (DEC_BATCH, DEC_SEQ, D_MODEL), 1.0),
        'cache_k': nrm(ks[7], (n_pool, PAGE_SIZE, N_HEADS, HEAD_DIM), 1.0),
        'cache_v': nrm(ks[8], (n_pool, PAGE_SIZE, N_HEADS, HEAD_DIM), 1.0),
        'page_table': page_table,
        'c_prompt': nrm(ks[9], (BATCH, D_MODEL), 1.0),
        'c_sample': nrm(ks[10], (DEC_BATCH, D_MODEL), 1.0),
        'w_ada': nrm(ks[11], (DEPTH, D_MODEL, 3 * D_MODEL), 0.5 * D_MODEL ** -0.5),
        'b_ada': nrm(ks[12], (DEPTH, 3 * D_MODEL), 0.02),
        'ln_g': 1.0 + nrm(ks[13], (DEPTH, D_MODEL), 0.02),
        'ln_b': nrm(ks[14], (DEPTH, D_MODEL), 0.02),
        'w_in_a': nrm(ks[15], (N_A, D_MODEL, 3 * D_INNER), D_MODEL ** -0.5),
        'sgu_g': 1.0 + nrm(ks[16], (N_A, D_INNER), 0.02),
        'sgu_b': nrm(ks[17], (N_A, D_INNER), 0.02),
        'w_s': nrm(ks[18], (N_A, N_GROUPS, CHUNK, CHUNK), CHUNK ** -0.5),
        'b_s': 1.0 + nrm(ks[19], (N_A, N_GROUPS, CHUNK), 0.02),
        'w_out_a': nrm(ks[20], (N_A, D_INNER, D_MODEL), BETA * D_INNER ** -0.5),
        'w_ada_kv': nrm(ks[21], (D_MODEL, 2 * D_MODEL), 0.5 * D_MODEL ** -0.5),
        'b_ada_kv': nrm(ks[22], (2 * D_MODEL,), 0.02),
        'w_kv': w_kv,
        'w_in_b': w_in_b,
        'w_out_b': nrm(ks[23], (N_B, ATT_W, D_MODEL), BETA * ATT_W ** -0.5),
        'sb_bias': SB_BIAS_INIT + nrm(ks[24], (N_B, N_HEADS), 0.1),
    }


def reference(x_prompt, x_sample, cache_k, cache_v, page_table, c_prompt, c_sample,
              w_ada, b_ada, ln_g, ln_b, w_in_a, sgu_g, sgu_b, w_s, b_s, w_out_a,
              w_ada_kv, b_ada_kv, w_kv, w_in_b, w_out_b, sb_bias):
    dec_b, n_pages = page_table.shape
    past_len = n_pages * cache_k.shape[1]

    def sb_sample(q, k, v, bias):
        past_k = cache_k[page_table].reshape(dec_b, past_len, N_HEADS, HEAD_DIM)
        past_v = cache_v[page_table].reshape(dec_b, past_len, N_HEADS, HEAD_DIM)
        kk = jnp.concatenate([past_k, k.astype(past_k.dtype)], axis=1)
        vv = jnp.concatenate([past_v, v.astype(past_v.dtype)], axis=1)
        t_new = q.shape[1]
        k_pos = jnp.arange(past_len + t_new, dtype=jnp.int32)
        q_pos = past_len + jnp.arange(t_new, dtype=jnp.int32)
        return stick_breaking(q, kk, vv, bias, q_pos, k_pos)

    y_prompt, k_prompt, v_prompt, _ = trunk(
        x_prompt, c_prompt, sb_prompt, w_ada, b_ada, ln_g, ln_b, w_in_a, sgu_g, sgu_b, w_s, b_s,
        w_out_a, w_ada_kv, b_ada_kv, w_kv, w_in_b, w_out_b, sb_bias)
    y_sample, k_sample, v_sample, chunk_v_sample = trunk(
        x_sample, c_sample, sb_sample, w_ada, b_ada, ln_g, ln_b, w_in_a, sgu_g, sgu_b, w_s, b_s,
        w_out_a, w_ada_kv, b_ada_kv, w_kv, w_in_b, w_out_b, sb_bias)
    return (y_prompt, y_sample, k_prompt, v_prompt, k_sample, v_sample, chunk_v_sample)
```

```python
import functools
import math

import jax
import jax.numpy as jnp
from jax import lax
from jax.experimental import pallas as pl
from jax.experimental.pallas import tpu as pltpu

F32 = jnp.float32
BF16 = jnp.bfloat16

HEAD_DIM = 64
CHUNK = 128
LN_EPS = 1e-5
LOG2E = math.log2(math.e)
LANES = 128
VMEM_LIMIT = 56 * 1024 * 1024

TM_DENSE = 512
TQ = 256
TK = 256
PAGES_PER_STEP = 8


def _silu(x):
    return x * (1.0 / (1.0 + jnp.exp(-x)))


def _ln(x, g, b):
    mu = jnp.mean(x, axis=-1, keepdims=True)
    xc = x - mu
    var = jnp.mean(xc * xc, axis=-1, keepdims=True)
    return xc * lax.rsqrt(var + LN_EPS) * g + b


def _dot(a, b):
    return jnp.dot(a, b, preferred_element_type=F32)


def _dot_nt(a, b):
    return lax.dot_general(a, b, (((1,), (1,)), ((), ())), preferred_element_type=F32)


def _dot_tn(a, b):
    return lax.dot_general(a, b, (((0,), (0,)), ((), ())), preferred_element_type=F32)


def _const_spec(shape):
    n = len(shape)
    return pl.BlockSpec(shape, lambda *_: (0,) * n, pipeline_mode=pl.Buffered(1))


def _ada_kernel(c_ref, w_ref, b_ref, o_ref):
    sc = _silu(c_ref[...]).astype(BF16)
    o_ref[...] = _dot(sc, w_ref[...].astype(BF16)) + b_ref[...]


def _ada(c, w, b):
    n_layers, d, n = w.shape
    r = c.shape[0]
    bn = 1024
    return pl.pallas_call(
        _ada_kernel,
        grid=(n_layers, n // bn),
        in_specs=[
            pl.BlockSpec((r, d), lambda l, j: (0, 0)),
            pl.BlockSpec((None, d, bn), lambda l, j: (l, 0, j)),
            pl.BlockSpec((None, 1, bn), lambda l, j: (l, 0, j)),
        ],
        out_specs=pl.BlockSpec((None, r, bn), lambda l, j: (l, 0, j)),
        out_shape=jax.ShapeDtypeStruct((n_layers, r, n), F32),
        name="ada",
    )(c, w, b.reshape(n_layers, 1, n))


def _layer0_kernel(x_ref, ada_ref, w_in_ref, sg_ref, sb_ref, ws_ref, bs_ref, w_out_ref,
                   lng_ref, lnb_ref, *rest, alpha, decode):
    d = x_ref.shape[-1]
    di = sg_ref.shape[-1]
    n_groups = ws_ref.shape[0]
    gd = di // n_groups
    x = x_ref[...]
    ada = ada_ref[0]
    shift, scale, gate = ada[:, 0:d], ada[:, d:2 * d], ada[:, 2 * d:3 * d]
    h = (x * (1.0 + scale) + shift).astype(BF16)
    vn = _ln(_dot(h, w_in_ref[:, di:2 * di]), sg_ref[...], sb_ref[...])
    if decode:
        o_ref, vn_ref = rest
        vn_ref[...] = vn
        mix = vn * ws_ref[...] + bs_ref[...]
        u = _dot(h, w_in_ref[:, 0:di])
        zg = _dot(h, w_in_ref[:, 2 * di:3 * di])
        gated = (u * mix * _silu(zg)).astype(BF16)
    else:
        o_ref, vn_scr, gated_scr = rest
        tm = x.shape[0]
        vn_scr[...] = vn.astype(BF16)
        row = lax.broadcasted_iota(jnp.int32, (CHUNK, CHUNK), 0)
        col = lax.broadcasted_iota(jnp.int32, (CHUNK, CHUNK), 1)
        causal = col <= row
        for g in range(n_groups):
            lo = g * gd
            wg = jnp.where(causal, ws_ref[g], 0.0).astype(BF16)
            bias = jnp.concatenate([bs_ref[g]] * (gd // LANES), axis=1)
            u = _dot(h, w_in_ref[:, lo:lo + gd])
            zg = _dot(h, w_in_ref[:, 2 * di + lo:2 * di + lo + gd])
            for c in range(tm // CHUNK):
                rows = slice(c * CHUNK, (c + 1) * CHUNK)
                mix = _dot(wg, vn_scr[rows, lo:lo + gd]) + bias
                gated_scr[rows, lo:lo + gd] = (u[rows] * mix * _silu(zg[rows])).astype(BF16)
        gated = gated_scr[...]
    y = _dot(gated, w_out_ref[...])
    o_ref[...] = _ln(alpha * x + gate * y, lng_ref[...], lnb_ref[...])


def _layer0(x, ada, w_in, sgu_g, sgu_b, w_s, b_s, w_out, ln_g, ln_b, *, alpha, decode):
    r, d = x.shape
    di = sgu_g.shape[-1]
    n_groups = w_s.shape[0]
    nb = ada.shape[0]
    if decode:
        tm = r
        ws_arg = jnp.repeat(w_s[:, 0, 0], di // n_groups)[None, :]
        bs_arg = jnp.repeat(b_s[:, 0], di // n_groups)[None, :]
        ws_spec = _const_spec((1, di))
        bs_spec = _const_spec((1, di))
        out_shape = (jax.ShapeDtypeStruct((r, d), F32), jax.ShapeDtypeStruct((r, di), F32))
        out_specs = (pl.BlockSpec((tm, d), lambda b, t: (b, 0)),
                     pl.BlockSpec((tm, di), lambda b, t: (b, 0)))
        scratch = []
        steps = 1
    else:
        tm = TM_DENSE
        ws_arg = w_s
        bs_arg = jnp.broadcast_to(b_s[:, :, None], (n_groups, CHUNK, LANES))
        ws_spec = _const_spec((n_groups, CHUNK, CHUNK))
        bs_spec = _const_spec((n_groups, CHUNK, LANES))
        out_shape = jax.ShapeDtypeStruct((r, d), F32)
        steps = r // nb // tm
        out_specs = pl.BlockSpec((tm, d), lambda b, t: (b * steps + t, 0))
        scratch = [pltpu.VMEM((tm, di), BF16), pltpu.VMEM((tm, di), BF16)]
    return pl.pallas_call(
        functools.partial(_layer0_kernel, alpha=alpha, decode=decode),
        grid=(nb, steps),
        in_specs=[
            pl.BlockSpec((tm, d), lambda b, t: (b * steps + t, 0)),
            pl.BlockSpec((1,) + ada.shape[1:], lambda b, t: (b, 0, 0)),
            _const_spec(w_in.shape),
            _const_spec((1, di)),
            _const_spec((1, di)),
            ws_spec,
            bs_spec,
            _const_spec(w_out.shape),
            _const_spec((1, d)),
            _const_spec((1, d)),
        ],
        out_specs=out_specs,
        out_shape=out_shape,
        scratch_shapes=scratch,
        compiler_params=pltpu.CompilerParams(
            dimension_semantics=("arbitrary", "arbitrary"), vmem_limit_bytes=VMEM_LIMIT),
        name="layer0_decode" if decode else "layer0",
    )(x, ada, w_in, sgu_g[None, :], sgu_b[None, :], ws_arg, bs_arg, w_out, ln_g[None, :], ln_b[None, :])


def _proj1_kernel(x_ref, ada_ref, akv_ref, w_kv_ref, w_in_ref,
                  k_ref, v_ref, kb_ref, vb_ref, q_ref, sz_ref, *, q_scale):
    d = x_ref.shape[-1]
    aw = k_ref.shape[-1]
    x = x_ref[...]
    ada = ada_ref[0]
    akv = akv_ref[0]
    hk = (x * (1.0 + akv[:, d:2 * d]) + akv[:, 0:d]).astype(BF16)
    kv = _dot(hk, w_kv_ref[...])
    k, v = kv[:, 0:aw], kv[:, aw:2 * aw]
    k_ref[...] = k
    v_ref[...] = v
    kb_ref[...] = k.astype(BF16)
    vb_ref[...] = v.astype(BF16)
    h = (x * (1.0 + ada[:, d:2 * d]) + ada[:, 0:d]).astype(BF16)
    qz = _dot(h, w_in_ref[...])
    q_ref[...] = (qz[:, 0:aw] * q_scale).astype(BF16)
    sz_ref[...] = _silu(qz[:, aw:2 * aw])


def _proj1(x, ada, akv, w_kv, w_in, *, tm):
    r, d = x.shape
    aw = w_kv.shape[1] // 2
    nb = ada.shape[0]
    steps = r // nb // tm
    row_spec = lambda n: pl.BlockSpec((tm, n), lambda b, t: (b * steps + t, 0))
    return pl.pallas_call(
        functools.partial(_proj1_kernel, q_scale=HEAD_DIM ** -0.5 * LOG2E),
        grid=(nb, steps),
        in_specs=[
            row_spec(d),
            pl.BlockSpec((1,) + ada.shape[1:], lambda b, t: (b, 0, 0)),
            pl.BlockSpec((1,) + akv.shape[1:], lambda b, t: (b, 0, 0)),
            _const_spec(w_kv.shape),
            _const_spec(w_in.shape),
        ],
        out_specs=(row_spec(aw),) * 6,
        out_shape=(
            jax.ShapeDtypeStruct((r, aw), F32), jax.ShapeDtypeStruct((r, aw), F32),
            jax.ShapeDtypeStruct((r, aw), BF16), jax.ShapeDtypeStruct((r, aw), BF16),
            jax.ShapeDtypeStruct((r, aw), BF16), jax.ShapeDtypeStruct((r, aw), F32),
        ),
        compiler_params=pltpu.CompilerParams(
            dimension_semantics=("arbitrary", "arbitrary"), vmem_limit_bytes=VMEM_LIMIT),
        name="proj1",
    )(x, ada, akv, w_kv, w_in)


def _sb_logs(z):
    e = jnp.exp2(-jnp.abs(z))
    l2 = jnp.log2(1.0 + e)
    lb = jnp.minimum(z, 0.0) - l2
    return lb, lb - z


def _split_bf16(x):
    hi = x.astype(BF16)
    lo = (x - hi.astype(F32)).astype(BF16)
    return hi, lo


def _sb_prompt_kernel(bias_ref, q_ref, k_ref, v_ref, o_ref, acc0_ref, acc1_ref):
    hp = pl.program_id(1)
    qi = pl.program_id(2)
    tq = q_ref.shape[1]
    q = q_ref[0].astype(F32)
    lane = lax.broadcasted_iota(jnp.int32, (tq, LANES), 1)
    first = lane < HEAD_DIM
    qs = (jnp.where(first, q, 0.0).astype(BF16), jnp.where(first, 0.0, q).astype(BF16))
    biases = (bias_ref[2 * hp] * LOG2E, bias_ref[2 * hp + 1] * LOG2E)
    accs = (acc0_ref, acc1_ref)
    r_i = lax.broadcasted_iota(jnp.int32, (TK, TK), 0)
    c_i = lax.broadcasted_iota(jnp.int32, (TK, TK), 1)
    later = jnp.where(r_i > c_i, 1.0, 0.0).astype(BF16)

    def tile(h, start, carry, mask):
        kblk = k_ref[0, pl.ds(start, TK), :]
        vblk = v_ref[0, pl.ds(start, TK), :]
        z = _dot_nt(qs[h], kblk) + biases[h]
        lb, l1 = _sb_logs(z)
        if mask is not None:
            l1 = jnp.where(mask, l1, 0.0)
        hi, lo = _split_bf16(l1)
        cs = _dot(hi, later) + _dot(lo, later)
        w = jnp.exp2(lb + cs + carry)
        if mask is not None:
            w = jnp.where(mask, w, 0.0)
        accs[h][...] += _dot(w.astype(BF16), vblk)
        return carry + cs[:, 0:1] + l1[:, 0:1]

    acc0_ref[...] = jnp.zeros_like(acc0_ref)
    acc1_ref[...] = jnp.zeros_like(acc1_ref)
    carries = [jnp.zeros((tq, 1), F32), jnp.zeros((tq, 1), F32)]
    q_pos = qi * tq + lax.broadcasted_iota(jnp.int32, (tq, TK), 0)
    k_off = lax.broadcasted_iota(jnp.int32, (tq, TK), 1)
    for dblk in reversed(range(tq // TK)):
        start = pl.multiple_of(qi * tq + dblk * TK, TK)
        mask = (start + k_off) < q_pos
        for h in range(2):
            carries[h] = tile(h, start, carries[h], mask)

    n_past = qi * (tq // TK)

    def body(i, carry):
        start = pl.multiple_of((n_past - 1 - i) * TK, TK)
        return tuple(tile(h, start, carry[h], None) for h in range(2))

    lax.fori_loop(0, n_past, body, tuple(carries))
    o_ref[0] = jnp.where(first, acc0_ref[...], acc1_ref[...])


def _sb_prompt(q, k, v, bias, *, bsz):
    r, aw = q.shape
    t = r // bsz
    q3, k3, v3 = (a.reshape(bsz, t, aw) for a in (q, k, v))
    out = pl.pallas_call(
        _sb_prompt_kernel,
        grid=(bsz, aw // LANES, t // TQ),
        in_specs=[
            pl.BlockSpec(memory_space=pltpu.SMEM),
            pl.BlockSpec((1, TQ, LANES), lambda b, hp, qi: (b, qi, hp)),
            pl.BlockSpec((1, t, LANES), lambda b, hp, qi: (b, 0, hp)),
            pl.BlockSpec((1, t, LANES), lambda b, hp, qi: (b, 0, hp)),
        ],
        out_specs=pl.BlockSpec((1, TQ, LANES), lambda b, hp, qi: (b, qi, hp)),
        scratch_shapes=[pltpu.VMEM((TQ, LANES), F32), pltpu.VMEM((TQ, LANES), F32)],
        out_shape=jax.ShapeDtypeStruct((bsz, t, aw), F32),
        compiler_params=pltpu.CompilerParams(
            dimension_semantics=("arbitrary", "arbitrary", "arbitrary"), vmem_limit_bytes=VMEM_LIMIT),
        name="sb_prompt",
    )(bias, q3, k3, v3)
    return out.reshape(r, aw)


def _sb_decode_kernel(pt_ref, q_ref, bias_ref, kn_ref, vn_ref, *rest, past_len):
    del pt_ref
    n = PAGES_PER_STEP
    k_refs, v_refs = rest[:n], rest[n:2 * n]
    o_ref, carry_ref, acc_ref = rest[2 * n:]
    g = pl.program_id(1)
    aw = q_ref.shape[-1]
    page = k_refs[0].shape[0]
    hrow = lax.broadcasted_iota(jnp.int32, (LANES, aw), 0)
    hlane = lax.broadcasted_iota(jnp.int32, (LANES, aw), 1) // HEAD_DIM
    own = hrow == hlane
    qb = jnp.broadcast_to(q_ref[0].astype(F32), (LANES, aw))
    qexp = jnp.where(own, qb, 0.0).astype(BF16)
    bias = bias_ref[...] * LOG2E

    def later_mat(n_keys):
        r_i = lax.broadcasted_iota(jnp.int32, (n_keys, n_keys), 0)
        c_i = lax.broadcasted_iota(jnp.int32, (n_keys, n_keys), 1)
        return jnp.where(c_i > r_i, 1.0, 0.0).astype(BF16)

    def tile(kf, vf, carry, mask):
        z = _dot_nt(kf.astype(BF16), qexp) + bias
        lb, l1 = _sb_logs(z)
        if mask is not None:
            l1 = jnp.where(mask, l1, 0.0)
        hi, lo = _split_bf16(l1)
        later = later_mat(kf.shape[0])
        cs = _dot(later, hi) + _dot(later, lo)
        w = jnp.exp2(lb + cs + carry)
        if mask is not None:
            w = jnp.where(mask, w, 0.0)
        acc_ref[...] += _dot_tn(w.astype(BF16), vf.astype(BF16))
        return carry + cs[0:1, :] + l1[0:1, :]

    @pl.when(g == 0)
    def _():
        acc_ref[...] = jnp.zeros_like(acc_ref)
        rows = page
        new_row = lax.broadcasted_iota(jnp.int32, (rows, LANES), 0)
        q_pos = past_len
        k_pos = past_len + new_row
        mask = (k_pos < q_pos) & (new_row < 1)
        kf = jnp.broadcast_to(kn_ref[0], (rows, aw))
        vf = jnp.broadcast_to(vn_ref[0], (rows, aw))
        carry_ref[...] = tile(kf, vf, jnp.zeros((1, LANES), F32), mask)

    carry = carry_ref[...]
    for i in range(n):
        carry = tile(k_refs[i][...], v_refs[i][...], carry, None)
    carry_ref[...] = carry

    @pl.when(g == pl.num_programs(1) - 1)
    def _():
        o_ref[0] = jnp.sum(jnp.where(own, acc_ref[...], 0.0), axis=0, keepdims=True)


def _sb_decode(q, k_new, v_new, bias, cache_k, cache_v, page_table):
    s, aw = q.shape
    n_pool, page = cache_k.shape[:2]
    n_pages = page_table.shape[1]
    n = PAGES_PER_STEP
    ck = cache_k.reshape(n_pool, page, aw)
    cv = cache_v.reshape(n_pool, page, aw)
    bias_row = jnp.zeros((1, LANES), F32).at[0, :bias.shape[0]].set(bias)

    def page_spec(i):
        return pl.BlockSpec(
            (None, page, aw),
            lambda b, g, pt: (pt[b * n_pages + n_pages - 1 - (g * n + i)], 0, 0))

    row_spec = pl.BlockSpec((1, 1, aw), lambda b, g, pt: (b, 0, 0))
    out = pl.pallas_call(
        functools.partial(_sb_decode_kernel, past_len=n_pages * page),
        grid_spec=pltpu.PrefetchScalarGridSpec(
            num_scalar_prefetch=1,
            grid=(s, n_pages // n),
            in_specs=[row_spec, pl.BlockSpec((1, LANES), lambda b, g, pt: (0, 0)), row_spec, row_spec]
            + [page_spec(i) for i in range(n)] * 2,
            out_specs=row_spec,
            scratch_shapes=[pltpu.VMEM((1, LANES), F32), pltpu.VMEM((LANES, aw), F32)],
        ),
        out_shape=jax.ShapeDtypeStruct((s, 1, aw), F32),
        compiler_params=pltpu.CompilerParams(
            dimension_semantics=("arbitrary", "arbitrary"), vmem_limit_bytes=VMEM_LIMIT),
        name="sb_decode",
    )(page_table.reshape(-1), q.reshape(s, 1, aw), bias_row, k_new.reshape(s, 1, aw),
      v_new.reshape(s, 1, aw), *([ck] * n), *([cv] * n))
    return out.reshape(s, aw)


def _out1_kernel(o_ref, sz_ref, x_ref, ada_ref, w_ref, lng_ref, lnb_ref, y_ref, *, alpha):
    d = x_ref.shape[-1]
    gated = (o_ref[...] * sz_ref[...]).astype(BF16)
    y = _dot(gated, w_ref[...])
    gate = ada_ref[0][:, 2 * d:3 * d]
    y_ref[...] = _ln(alpha * x_ref[...] + gate * y, lng_ref[...], lnb_ref[...])


def _out1(o, sz, x, ada, w_out, ln_g, ln_b, *, alpha, tm):
    r, d = x.shape
    aw = o.shape[1]
    nb = ada.shape[0]
    steps = r // nb // tm
    row_spec = lambda n: pl.BlockSpec((tm, n), lambda b, t: (b * steps + t, 0))
    return pl.pallas_call(
        functools.partial(_out1_kernel, alpha=alpha),
        grid=(nb, steps),
        in_specs=[
            row_spec(aw), row_spec(aw), row_spec(d),
            pl.BlockSpec((1,) + ada.shape[1:], lambda b, t: (b, 0, 0)),
            _const_spec(w_out.shape), _const_spec((1, d)), _const_spec((1, d)),
        ],
        out_specs=row_spec(d),
        out_shape=jax.ShapeDtypeStruct((r, d), F32),
        compiler_params=pltpu.CompilerParams(
            dimension_semantics=("arbitrary", "arbitrary"), vmem_limit_bytes=VMEM_LIMIT),
        name="out1",
    )(o, sz, x, ada, w_out, ln_g[None, :], ln_b[None, :])


def kernel(x_prompt, x_sample, cache_k, cache_v, page_table, c_prompt, c_sample, w_ada, b_ada, ln_g, ln_b, w_in_a, sgu_g, sgu_b, w_s, b_s, w_out_a, w_ada_kv, b_ada_kv, w_kv, w_in_b, w_out_b, sb_bias):
    bsz, seq, d = x_prompt.shape
    dec_b, dec_seq, _ = x_sample.shape
    depth = w_ada.shape[0]
    assert depth == 2 and w_in_a.shape[0] == 1 and w_in_b.shape[0] == 1
    assert dec_seq == 1 and seq % TM_DENSE == 0 and seq % TQ == 0 and TQ % TK == 0
    assert page_table.shape[1] % PAGES_PER_STEP == 0
    n_heads = sb_bias.shape[1]
    alpha = (2 * depth) ** 0.25

    c_all = jnp.concatenate([c_prompt, c_sample], axis=0)
    ada = _ada(c_all, w_ada, b_ada)
    akv = _ada(c_all, w_ada_kv[None], b_ada_kv[None])[0]
    split = lambda a: (a[:bsz, None, :], a[None, bsz:, :])
    ada0_p, ada0_s = split(ada[0])
    ada1_p, ada1_s = split(ada[1])
    akv_p, akv_s = split(akv)

    w_in_a_b = w_in_a[0].astype(BF16)
    w_out_a_b = w_out_a[0].astype(BF16)
    w_kv_b = w_kv.astype(BF16)
    w_in_b_b = w_in_b[0].astype(BF16)
    w_out_b_b = w_out_b[0].astype(BF16)
    l0 = functools.partial(_layer0, w_in=w_in_a_b, sgu_g=sgu_g[0], sgu_b=sgu_b[0], w_s=w_s[0], b_s=b_s[0],
                           w_out=w_out_a_b, ln_g=ln_g[0], ln_b=ln_b[0], alpha=alpha)

    xp = x_prompt.reshape(bsz * seq, d)
    x1p = l0(xp, ada0_p, decode=False)
    kp, vp, kpb, vpb, qp, szp = _proj1(x1p, ada1_p, akv_p, w_kv_b, w_in_b_b, tm=TM_DENSE)
    op = _sb_prompt(qp, kpb, vpb, sb_bias[0], bsz=bsz)
    yp = _out1(op, szp, x1p, ada1_p, w_out_b_b, ln_g[1], ln_b[1], alpha=alpha, tm=TM_DENSE)

    xs = x_sample.reshape(dec_b * dec_seq, d)
    x1s, vns = l0(xs, ada0_s, decode=True)
    ks, vs, _, _, qs, szs = _proj1(x1s, ada1_s, akv_s, w_kv_b, w_in_b_b, tm=dec_b)
    osamp = _sb_decode(qs, ks, vs, sb_bias[0], cache_k, cache_v, page_table)
    ys = _out1(osamp, szs, x1s, ada1_s, w_out_b_b, ln_g[1], ln_b[1], alpha=alpha, tm=dec_b)

    hd = (n_heads, d // n_heads)
    return (yp.reshape(bsz, seq, d), ys.reshape(dec_b, dec_seq, d),
            kp.reshape(bsz, seq, *hd), vp.reshape(bsz, seq, *hd),
            ks.reshape(dec_b, dec_seq, *hd), vs.reshape(dec_b, dec_seq, *hd),
            vns.reshape(1, dec_b, dec_seq, -1))
```

```python
import functools
import math

import jax
import jax.numpy as jnp
from jax import lax
from jax.experimental import pallas as pl
from jax.experimental.pallas import tpu as pltpu

F32 = jnp.float32
BF16 = jnp.bfloat16

HEAD_DIM = 64
CHUNK = 128
LN_EPS = 1e-5
LOG2E = math.log2(math.e)
LANES = 128
VMEM_LIMIT = 56 * 1024 * 1024

TM_DENSE = 512
TQ = 256
TK = 256
HEAD_BLOCK = 256
PAGES_PER_STEP = 8


def _silu(x):
    return x * (1.0 / (1.0 + jnp.exp(-x)))


def _ln(x, g, b):
    mu = jnp.mean(x, axis=-1, keepdims=True)
    xc = x - mu
    var = jnp.mean(xc * xc, axis=-1, keepdims=True)
    return xc * lax.rsqrt(var + LN_EPS) * g + b


def _dot(a, b):
    return jnp.dot(a, b, preferred_element_type=F32)


def _dot_nt(a, b):
    return lax.dot_general(a, b, (((1,), (1,)), ((), ())), preferred_element_type=F32)


def _dot_tn(a, b):
    return lax.dot_general(a, b, (((0,), (0,)), ((), ())), preferred_element_type=F32)


def _const_spec(shape):
    n = len(shape)
    return pl.BlockSpec(shape, lambda *_: (0,) * n, pipeline_mode=pl.Buffered(1))


def _ada_kernel(c_ref, w_ref, b_ref, o_ref):
    sc = _silu(c_ref[...]).astype(BF16)
    o_ref[...] = _dot(sc, w_ref[...].astype(BF16)) + b_ref[...]


def _ada(c, w, b):
    n_layers, d, n = w.shape
    r = c.shape[0]
    bn = 1024
    return pl.pallas_call(
        _ada_kernel,
        grid=(n_layers, n // bn),
        in_specs=[
            pl.BlockSpec((r, d), lambda l, j: (0, 0)),
            pl.BlockSpec((None, d, bn), lambda l, j: (l, 0, j)),
            pl.BlockSpec((None, 1, bn), lambda l, j: (l, 0, j)),
        ],
        out_specs=pl.BlockSpec((None, r, bn), lambda l, j: (l, 0, j)),
        out_shape=jax.ShapeDtypeStruct((n_layers, r, n), F32),
        name="ada",
    )(c, w, b.reshape(n_layers, 1, n))


def _layer0_kernel(x_ref, ada_ref, w_in_ref, sg_ref, sb_ref, ws_ref, bs_ref, w_out_ref,
                   lng_ref, lnb_ref, *rest, alpha, decode):
    d = x_ref.shape[-1]
    di = sg_ref.shape[-1]
    n_groups = ws_ref.shape[0]
    gd = di // n_groups
    x = x_ref[...]
    ada = ada_ref[0]
    shift, scale, gate = ada[:, 0:d], ada[:, d:2 * d], ada[:, 2 * d:3 * d]
    h = (x * (1.0 + scale) + shift).astype(BF16)
    vn = _ln(_dot(h, w_in_ref[:, di:2 * di]), sg_ref[...], sb_ref[...])
    if decode:
        o_ref, vn_ref = rest
        vn_ref[...] = vn
        mix = vn * ws_ref[...] + bs_ref[...]
        u = _dot(h, w_in_ref[:, 0:di])
        zg = _dot(h, w_in_ref[:, 2 * di:3 * di])
        gated = (u * mix * _silu(zg)).astype(BF16)
    else:
        o_ref, vn_scr, gated_scr = rest
        tm = x.shape[0]
        vn_scr[...] = vn.astype(BF16)
        row = lax.broadcasted_iota(jnp.int32, (CHUNK, CHUNK), 0)
        col = lax.broadcasted_iota(jnp.int32, (CHUNK, CHUNK), 1)
        causal = col <= row
        for g in range(n_groups):
            lo = g * gd
            wg = jnp.where(causal, ws_ref[g], 0.0).astype(BF16)
            bias = jnp.concatenate([bs_ref[g]] * (gd // LANES), axis=1)
            u = _dot(h, w_in_ref[:, lo:lo + gd])
            zg = _dot(h, w_in_ref[:, 2 * di + lo:2 * di + lo + gd])
            for c in range(tm // CHUNK):
                rows = slice(c * CHUNK, (c + 1) * CHUNK)
                mix = _dot(wg, vn_scr[rows, lo:lo + gd]) + bias
                gated_scr[rows, lo:lo + gd] = (u[rows] * mix * _silu(zg[rows])).astype(BF16)
        gated = gated_scr[...]
    y = _dot(gated, w_out_ref[...])
    o_ref[...] = _ln(alpha * x + gate * y, lng_ref[...], lnb_ref[...])


def _layer0(x, ada, w_in, sgu_g, sgu_b, w_s, b_s, w_out, ln_g, ln_b, *, alpha, decode):
    r, d = x.shape
    di = sgu_g.shape[-1]
    n_groups = w_s.shape[0]
    nb = ada.shape[0]
    if decode:
        tm = r
        ws_arg = jnp.repeat(w_s[:, 0, 0], di // n_groups)[None, :]
        bs_arg = jnp.repeat(b_s[:, 0], di // n_groups)[None, :]
        ws_spec = _const_spec((1, di))
        bs_spec = _const_spec((1, di))
        out_shape = (jax.ShapeDtypeStruct((r, d), F32), jax.ShapeDtypeStruct((r, di), F32))
        out_specs = (pl.BlockSpec((tm, d), lambda b, t: (b, 0)),
                     pl.BlockSpec((tm, di), lambda b, t: (b, 0)))
        scratch = []
        steps = 1
    else:
        tm = TM_DENSE
        ws_arg = w_s
        bs_arg = jnp.broadcast_to(b_s[:, :, None], (n_groups, CHUNK, LANES))
        ws_spec = _const_spec((n_groups, CHUNK, CHUNK))
        bs_spec = _const_spec((n_groups, CHUNK, LANES))
        out_shape = jax.ShapeDtypeStruct((r, d), F32)
        steps = r // nb // tm
        out_specs = pl.BlockSpec((tm, d), lambda b, t: (b * steps + t, 0))
        scratch = [pltpu.VMEM((tm, di), BF16), pltpu.VMEM((tm, di), BF16)]
    return pl.pallas_call(
        functools.partial(_layer0_kernel, alpha=alpha, decode=decode),
        grid=(nb, steps),
        in_specs=[
            pl.BlockSpec((tm, d), lambda b, t: (b * steps + t, 0)),
            pl.BlockSpec((1,) + ada.shape[1:], lambda b, t: (b, 0, 0)),
            _const_spec(w_in.shape),
            _const_spec((1, di)),
            _const_spec((1, di)),
            ws_spec,
            bs_spec,
            _const_spec(w_out.shape),
            _const_spec((1, d)),
            _const_spec((1, d)),
        ],
        out_specs=out_specs,
        out_shape=out_shape,
        scratch_shapes=scratch,
        compiler_params=pltpu.CompilerParams(
            dimension_semantics=("arbitrary", "arbitrary"), vmem_limit_bytes=VMEM_LIMIT),
        name="layer0_decode" if decode else "layer0",
    )(x, ada, w_in, sgu_g[None, :], sgu_b[None, :], ws_arg, bs_arg, w_out, ln_g[None, :], ln_b[None, :])


def _proj1_kernel(x_ref, ada_ref, akv_ref, w_kv_ref, w_in_ref,
                  k_ref, v_ref, kb_ref, vb_ref, q_ref, sz_ref, *, q_scale):
    d = x_ref.shape[-1]
    aw = k_ref.shape[-1]
    x = x_ref[...]
    ada = ada_ref[0]
    akv = akv_ref[0]
    hk = (x * (1.0 + akv[:, d:2 * d]) + akv[:, 0:d]).astype(BF16)
    kv = _dot(hk, w_kv_ref[...])
    k, v = kv[:, 0:aw], kv[:, aw:2 * aw]
    k_ref[...] = k
    v_ref[...] = v
    kb_ref[...] = k.astype(BF16)
    vb_ref[...] = v.astype(BF16)
    h = (x * (1.0 + ada[:, d:2 * d]) + ada[:, 0:d]).astype(BF16)
    qz = _dot(h, w_in_ref[...])
    q_ref[...] = (qz[:, 0:aw] * q_scale).astype(BF16)
    sz_ref[...] = _silu(qz[:, aw:2 * aw])


def _proj1(x, ada, akv, w_kv, w_in, *, tm):
    r, d = x.shape
    aw = w_kv.shape[1] // 2
    nb = ada.shape[0]
    steps = r // nb // tm
    row_spec = lambda n: pl.BlockSpec((tm, n), lambda b, t: (b * steps + t, 0))
    return pl.pallas_call(
        functools.partial(_proj1_kernel, q_scale=HEAD_DIM ** -0.5 * LOG2E),
        grid=(nb, steps),
        in_specs=[
            row_spec(d),
            pl.BlockSpec((1,) + ada.shape[1:], lambda b, t: (b, 0, 0)),
            pl.BlockSpec((1,) + akv.shape[1:], lambda b, t: (b, 0, 0)),
            _const_spec(w_kv.shape),
            _const_spec(w_in.shape),
        ],
        out_specs=(row_spec(aw),) * 6,
        out_shape=(
            jax.ShapeDtypeStruct((r, aw), F32), jax.ShapeDtypeStruct((r, aw), F32),
            jax.ShapeDtypeStruct((r, aw), BF16), jax.ShapeDtypeStruct((r, aw), BF16),
            jax.ShapeDtypeStruct((r, aw), BF16), jax.ShapeDtypeStruct((r, aw), F32),
        ),
        compiler_params=pltpu.CompilerParams(
            dimension_semantics=("arbitrary", "arbitrary"), vmem_limit_bytes=VMEM_LIMIT),
        name="proj1",
    )(x, ada, akv, w_kv, w_in)


def _sb_logs(z):
    sign = jnp.int32(-2 ** 31)
    e = jnp.exp2(lax.bitcast_convert_type(lax.bitcast_convert_type(z, jnp.int32) | sign, F32))
    l2 = jnp.log2(1.0 + e)
    lb = jnp.minimum(z, 0.0) - l2
    return lb, lb - z


def _split_bf16(x):
    hi = x.astype(BF16)
    lo = (x - hi.astype(F32)).astype(BF16)
    return hi, lo


def _sb_prompt_kernel(bias_ref, q_ref, k_ref, v_ref, o_ref, acc_ref):
    hb = pl.program_id(1)
    qi = pl.program_id(2)
    tq, width = q_ref.shape[1], q_ref.shape[2]
    n_heads = width // HEAD_DIM
    per_tile = LANES // HEAD_DIM
    lane = lax.broadcasted_iota(jnp.int32, (tq, LANES), 1)
    lane_head = lane // HEAD_DIM
    qs, biases = [], []
    for h in range(n_heads):
        lt = h // per_tile
        q_tile = q_ref[0, :, lt * LANES:(lt + 1) * LANES].astype(F32)
        qs.append(jnp.where(lane_head == h % per_tile, q_tile, 0.0).astype(BF16))
        biases.append(bias_ref[hb * n_heads + h] * LOG2E)
    r_i = lax.broadcasted_iota(jnp.int32, (TK, TK), 0)
    c_i = lax.broadcasted_iota(jnp.int32, (TK, TK), 1)
    later = jnp.where(r_i > c_i, 1.0, 0.0).astype(BF16)

    def tiles(start, carries, mask):
        heads = range(n_heads)
        cols = [slice((h // per_tile) * LANES, (h // per_tile + 1) * LANES) for h in heads]
        zs = [_dot_nt(qs[h], k_ref[0, pl.ds(start, TK), cols[h]]) + biases[h] for h in heads]
        logs = [_sb_logs(z) for z in zs]
        l1s = [l1 if mask is None else jnp.where(mask, l1, 0.0) for _, l1 in logs]
        css = [_dot(l1.astype(BF16), later) for l1 in l1s]
        new_carries = []
        for h in heads:
            w = jnp.exp2(logs[h][0] + carries[h] + css[h])
            if mask is not None:
                w = jnp.where(mask, w, 0.0)
            acc_ref[h] += _dot(w.astype(BF16), v_ref[0, pl.ds(start, TK), cols[h]])
            new_carries.append(carries[h] + jnp.sum(l1s[h], axis=1, keepdims=True))
        return new_carries

    acc_ref[...] = jnp.zeros_like(acc_ref)
    carries = [jnp.zeros((tq, 1), F32)] * n_heads
    q_pos = qi * tq + lax.broadcasted_iota(jnp.int32, (tq, TK), 0)
    k_off = lax.broadcasted_iota(jnp.int32, (tq, TK), 1)
    for dblk in reversed(range(tq // TK)):
        start = pl.multiple_of(qi * tq + dblk * TK, TK)
        carries = tiles(start, carries, (start + k_off) < q_pos)

    n_past = qi * (tq // TK)

    def body(i, carry):
        start = pl.multiple_of((n_past - 1 - i) * TK, TK)
        return tuple(tiles(start, list(carry), None))

    lax.fori_loop(0, n_past, body, tuple(carries))
    first = lane_head == 0
    o_ref[0] = jnp.concatenate(
        [jnp.where(first, acc_ref[2 * lt], acc_ref[2 * lt + 1]) for lt in range(width // LANES)], axis=1)


def _sb_prompt(q, k, v, bias, *, bsz):
    r, aw = q.shape
    t = r // bsz
    q3, k3, v3 = (a.reshape(bsz, t, aw) for a in (q, k, v))
    out = pl.pallas_call(
        _sb_prompt_kernel,
        grid=(bsz, aw // HEAD_BLOCK, t // TQ),
        in_specs=[
            pl.BlockSpec(memory_space=pltpu.SMEM),
            pl.BlockSpec((1, TQ, HEAD_BLOCK), lambda b, hb, qi: (b, qi, hb)),
            pl.BlockSpec((1, t, HEAD_BLOCK), lambda b, hb, qi: (b, 0, hb)),
            pl.BlockSpec((1, t, HEAD_BLOCK), lambda b, hb, qi: (b, 0, hb)),
        ],
        out_specs=pl.BlockSpec((1, TQ, HEAD_BLOCK), lambda b, hb, qi: (b, qi, hb)),
        scratch_shapes=[pltpu.VMEM((HEAD_BLOCK // HEAD_DIM, TQ, LANES), F32)],
        out_shape=jax.ShapeDtypeStruct((bsz, t, aw), F32),
        compiler_params=pltpu.CompilerParams(
            dimension_semantics=("arbitrary", "arbitrary", "arbitrary"), vmem_limit_bytes=VMEM_LIMIT),
        name="sb_prompt",
    )(bias, q3, k3, v3)
    return out.reshape(r, aw)


def _sb_decode_kernel(pt_ref, q_ref, bias_ref, kn_ref, vn_ref, *rest, past_len):
    del pt_ref
    n = PAGES_PER_STEP
    k_refs, v_refs = rest[:n], rest[n:2 * n]
    o_ref, carry_ref, acc_ref = rest[2 * n:]
    g = pl.program_id(1)
    page, n_heads, hd = k_refs[0].shape
    per_row = LANES // n_heads
    n_page_rows = page // per_row
    q = q_ref[0]
    bias = bias_ref[...] * LOG2E
    lane_head = lax.broadcasted_iota(jnp.int32, (n_heads, LANES), 1) % n_heads
    own = lane_head == lax.broadcasted_iota(jnp.int32, (n_heads, LANES), 0)
    l_a = lax.broadcasted_iota(jnp.int32, (LANES, LANES), 0)
    l_b = lax.broadcasted_iota(jnp.int32, (LANES, LANES), 1)
    same_head = (l_a % n_heads) == (l_b % n_heads)
    group = jnp.where(same_head, 1.0, 0.0).astype(BF16)
    later = jnp.where(same_head & (l_a // n_heads > l_b // n_heads), 1.0, 0.0).astype(BF16)

    def tile(k3s, v3s, carry, mask):
        n_rows = n_page_rows * len(k3s)
        k2 = jnp.concatenate([k3.reshape(page * n_heads, hd).astype(BF16) for k3 in k3s], axis=0)
        v2 = jnp.concatenate([v3.reshape(page * n_heads, hd).astype(BF16) for v3 in v3s], axis=0)
        zt = _dot_nt(q, k2)
        z = jnp.concatenate(
            [jnp.sum(jnp.where(own, zt[:, j * LANES:(j + 1) * LANES], 0.0), axis=0, keepdims=True)
             for j in range(n_rows)], axis=0) + bias
        lb, l1 = _sb_logs(z)
        if mask is not None:
            l1 = jnp.where(mask, l1, 0.0)
        hl = jnp.concatenate(_split_bf16(l1), axis=0)
        within = _dot(hl, later)
        within = within[0:n_rows] + within[n_rows:2 * n_rows]
        totals = _dot(hl, group)
        totals = totals[0:n_rows] + totals[n_rows:2 * n_rows]
        r_a = lax.broadcasted_iota(jnp.int32, (n_rows, 2 * n_rows), 0)
        r_b = lax.broadcasted_iota(jnp.int32, (n_rows, 2 * n_rows), 1) % n_rows
        newer_rows = jnp.where(r_b > r_a, 1.0, 0.0).astype(BF16)
        newer = _dot(newer_rows, jnp.concatenate(_split_bf16(totals), axis=0))
        w = jnp.exp2(lb + within + newer + carry)
        if mask is not None:
            w = jnp.where(mask, w, 0.0)
        w_rows = jnp.concatenate(
            [jnp.where(own, jnp.broadcast_to(w[j:j + 1], (n_heads, LANES)), 0.0) for j in range(n_rows)],
            axis=1).astype(BF16)
        acc_ref[...] += _dot(w_rows, v2)
        return carry + newer[0:1] + totals[0:1]

    @pl.when(g == 0)
    def _():
        acc_ref[...] = jnp.zeros_like(acc_ref)
        key = (lax.broadcasted_iota(jnp.int32, (n_page_rows, LANES), 0) * per_row
               + lax.broadcasted_iota(jnp.int32, (n_page_rows, LANES), 1) // n_heads)
        mask = ((past_len + key) < past_len) & (key < 1)
        k3 = jnp.broadcast_to(kn_ref[...], (page, n_heads, hd))
        v3 = jnp.broadcast_to(vn_ref[...], (page, n_heads, hd))
        carry_ref[...] = tile([k3], [v3], jnp.zeros((1, LANES), F32), mask)

    carry_ref[...] = tile([r[...] for r in reversed(k_refs)], [r[...] for r in reversed(v_refs)],
                          carry_ref[...], None)

    @pl.when(g == pl.num_programs(1) - 1)
    def _():
        o_ref[0] = acc_ref[...]


def _sb_decode(q, k_new, v_new, bias, cache_k, cache_v, page_table):
    s, aw = q.shape
    n_pool, page, n_heads, hd = cache_k.shape
    n_pages = page_table.shape[1]
    n = PAGES_PER_STEP
    bias_row = jnp.tile(bias, LANES // n_heads)[None, :]

    def page_spec(i):
        return pl.BlockSpec(
            (None, page, n_heads, hd),
            lambda b, g, pt: (pt[b * n_pages + n_pages - 1 - (g * n + i)], 0, 0, 0))

    head_spec = pl.BlockSpec((1, n_heads, hd), lambda b, g, pt: (b, 0, 0))
    out = pl.pallas_call(
        functools.partial(_sb_decode_kernel, past_len=n_pages * page),
        grid_spec=pltpu.PrefetchScalarGridSpec(
            num_scalar_prefetch=1,
            grid=(s, n_pages // n),
            in_specs=[head_spec, pl.BlockSpec((1, LANES), lambda b, g, pt: (0, 0)), head_spec, head_spec]
            + [page_spec(i) for i in range(n)] * 2,
            out_specs=head_spec,
            scratch_shapes=[pltpu.VMEM((1, LANES), F32), pltpu.VMEM((n_heads, hd), F32)],
        ),
        out_shape=jax.ShapeDtypeStruct((s, n_heads, hd), F32),
        compiler_params=pltpu.CompilerParams(
            dimension_semantics=("arbitrary", "arbitrary"), vmem_limit_bytes=VMEM_LIMIT),
        name="sb_decode",
    )(page_table.reshape(-1), q.reshape(s, n_heads, hd), bias_row, k_new.reshape(s, n_heads, hd),
      v_new.reshape(s, n_heads, hd), *([cache_k] * n), *([cache_v] * n))
    return out.reshape(s, aw)


def _out1_kernel(o_ref, sz_ref, x_ref, ada_ref, w_ref, lng_ref, lnb_ref, y_ref, *, alpha):
    d = x_ref.shape[-1]
    gated = (o_ref[...] * sz_ref[...]).astype(BF16)
    y = _dot(gated, w_ref[...])
    gate = ada_ref[0][:, 2 * d:3 * d]
    y_ref[...] = _ln(alpha * x_ref[...] + gate * y, lng_ref[...], lnb_ref[...])


def _out1(o, sz, x, ada, w_out, ln_g, ln_b, *, alpha, tm):
    r, d = x.shape
    aw = o.shape[1]
    nb = ada.shape[0]
    steps = r // nb // tm
    row_spec = lambda n: pl.BlockSpec((tm, n), lambda b, t: (b * steps + t, 0))
    return pl.pallas_call(
        functools.partial(_out1_kernel, alpha=alpha),
        grid=(nb, steps),
        in_specs=[
            row_spec(aw), row_spec(aw), row_spec(d),
            pl.BlockSpec((1,) + ada.shape[1:], lambda b, t: (b, 0, 0)),
            _const_spec(w_out.shape), _const_spec((1, d)), _const_spec((1, d)),
        ],
        out_specs=row_spec(d),
        out_shape=jax.ShapeDtypeStruct((r, d), F32),
        compiler_params=pltpu.CompilerParams(
            dimension_semantics=("arbitrary", "arbitrary"), vmem_limit_bytes=VMEM_LIMIT),
        name="out1",
    )(o, sz, x, ada, w_out, ln_g[None, :], ln_b[None, :])


def kernel(x_prompt, x_sample, cache_k, cache_v, page_table, c_prompt, c_sample, w_ada, b_ada, ln_g, ln_b, w_in_a, sgu_g, sgu_b, w_s, b_s, w_out_a, w_ada_kv, b_ada_kv, w_kv, w_in_b, w_out_b, sb_bias):
    bsz, seq, d = x_prompt.shape
    dec_b, dec_seq, _ = x_sample.shape
    depth = w_ada.shape[0]
    assert depth == 2 and w_in_a.shape[0] == 1 and w_in_b.shape[0] == 1
    assert dec_seq == 1 and seq % TM_DENSE == 0 and seq % TQ == 0 and TQ % TK == 0
    assert page_table.shape[1] % PAGES_PER_STEP == 0
    n_heads = sb_bias.shape[1]
    alpha = (2 * depth) ** 0.25

    c_all = jnp.concatenate([c_prompt, c_sample], axis=0)
    ada = _ada(c_all, w_ada, b_ada)
    akv = _ada(c_all, w_ada_kv[None], b_ada_kv[None])[0]
    split = lambda a: (a[:bsz, None, :], a[None, bsz:, :])
    ada0_p, ada0_s = split(ada[0])
    ada1_p, ada1_s = split(ada[1])
    akv_p, akv_s = split(akv)

    w_in_a_b = w_in_a[0].astype(BF16)
    w_out_a_b = w_out_a[0].astype(BF16)
    w_kv_b = w_kv.astype(BF16)
    w_in_b_b = w_in_b[0].astype(BF16)
    w_out_b_b = w_out_b[0].astype(BF16)
    l0 = functools.partial(_layer0, w_in=w_in_a_b, sgu_g=sgu_g[0], sgu_b=sgu_b[0], w_s=w_s[0], b_s=b_s[0],
                           w_out=w_out_a_b, ln_g=ln_g[0], ln_b=ln_b[0], alpha=alpha)

    xp = x_prompt.reshape(bsz * seq, d)
    x1p = l0(xp, ada0_p, decode=False)
    kp, vp, kpb, vpb, qp, szp = _proj1(x1p, ada1_p, akv_p, w_kv_b, w_in_b_b, tm=TM_DENSE)
    op = _sb_prompt(qp, kpb, vpb, sb_bias[0], bsz=bsz)
    yp = _out1(op, szp, x1p, ada1_p, w_out_b_b, ln_g[1], ln_b[1], alpha=alpha, tm=TM_DENSE)

    xs = x_sample.reshape(dec_b * dec_seq, d)
    x1s, vns = l0(xs, ada0_s, decode=True)
    ks, vs, _, _, qs, szs = _proj1(x1s, ada1_s, akv_s, w_kv_b, w_in_b_b, tm=dec_b)
    osamp = _sb_decode(qs, ks, vs, sb_bias[0], cache_k, cache_v, page_table)
    ys = _out1(osamp, szs, x1s, ada1_s, w_out_b_b, ln_g[1], ln_b[1], alpha=alpha, tm=dec_b)

    hd = (n_heads, d // n_heads)
    return (yp.reshape(bsz, seq, d), ys.reshape(dec_b, dec_seq, d),
            kp.reshape(bsz, seq, *hd), vp.reshape(bsz, seq, *hd),
            ks.reshape(dec_b, dec_seq, *hd), vs.reshape(dec_b, dec_seq, *hd),
            vns.reshape(1, dec_b, dec_seq, -1))
```

```python
import functools
import math

import jax
import jax.numpy as jnp
from jax import lax
from jax.experimental import pallas as pl
from jax.experimental.pallas import tpu as pltpu

F32 = jnp.float32
BF16 = jnp.bfloat16

HEAD_DIM = 64
CHUNK = 128
LN_EPS = 1e-5
LOG2E = math.log2(math.e)
LANES = 128
VMEM_LIMIT = 56 * 1024 * 1024

TM_DENSE = 512
TQ = 256
TK = 256
HEAD_BLOCK = 256
PAGES_PER_STEP = 8


def _silu(x):
    return x * (1.0 / (1.0 + jnp.exp(-x)))


def _ln(x, g, b):
    mu = jnp.mean(x, axis=-1, keepdims=True)
    xc = x - mu
    var = jnp.mean(xc * xc, axis=-1, keepdims=True)
    return xc * lax.rsqrt(var + LN_EPS) * g + b


def _dot(a, b):
    return jnp.dot(a, b, preferred_element_type=F32)


def _dot_nt(a, b):
    return lax.dot_general(a, b, (((1,), (1,)), ((), ())), preferred_element_type=F32)


def _dot_tn(a, b):
    return lax.dot_general(a, b, (((0,), (0,)), ((), ())), preferred_element_type=F32)


def _const_spec(shape):
    n = len(shape)
    return pl.BlockSpec(shape, lambda *_: (0,) * n, pipeline_mode=pl.Buffered(1))


def _ada_kernel(c_ref, w_ref, b_ref, o_ref):
    sc = _silu(c_ref[...]).astype(BF16)
    o_ref[...] = _dot(sc, w_ref[...].astype(BF16)) + b_ref[...]


def _ada(c, w, b):
    n_layers, d, n = w.shape
    r = c.shape[0]
    bn = 1024
    return pl.pallas_call(
        _ada_kernel,
        grid=(n_layers, n // bn),
        in_specs=[
            pl.BlockSpec((r, d), lambda l, j: (0, 0)),
            pl.BlockSpec((None, d, bn), lambda l, j: (l, 0, j)),
            pl.BlockSpec((None, 1, bn), lambda l, j: (l, 0, j)),
        ],
        out_specs=pl.BlockSpec((None, r, bn), lambda l, j: (l, 0, j)),
        out_shape=jax.ShapeDtypeStruct((n_layers, r, n), F32),
        name="ada",
    )(c, w, b.reshape(n_layers, 1, n))


def _layer0_kernel(x_ref, ada_ref, w_in_ref, sg_ref, sb_ref, ws_ref, bs_ref, w_out_ref,
                   lng_ref, lnb_ref, *rest, alpha, decode):
    d = x_ref.shape[-1]
    di = sg_ref.shape[-1]
    n_groups = ws_ref.shape[0]
    gd = di // n_groups
    x = x_ref[...]
    ada = ada_ref[0]
    shift, scale, gate = ada[:, 0:d], ada[:, d:2 * d], ada[:, 2 * d:3 * d]
    h = (x * (1.0 + scale) + shift).astype(BF16)
    vn = _ln(_dot(h, w_in_ref[:, di:2 * di]), sg_ref[...], sb_ref[...])
    if decode:
        o_ref, vn_ref = rest
        vn_ref[...] = vn
        mix = vn * ws_ref[...] + bs_ref[...]
        u = _dot(h, w_in_ref[:, 0:di])
        zg = _dot(h, w_in_ref[:, 2 * di:3 * di])
        gated = (u * mix * _silu(zg)).astype(BF16)
    else:
        o_ref, vn_scr, gated_scr = rest
        tm = x.shape[0]
        vn_scr[...] = vn.astype(BF16)
        row = lax.broadcasted_iota(jnp.int32, (CHUNK, CHUNK), 0)
        col = lax.broadcasted_iota(jnp.int32, (CHUNK, CHUNK), 1)
        causal = col <= row
        for g in range(n_groups):
            lo = g * gd
            wg = jnp.where(causal, ws_ref[g], 0.0).astype(BF16)
            bias = jnp.concatenate([bs_ref[g]] * (gd // LANES), axis=1)
            u = _dot(h, w_in_ref[:, lo:lo + gd])
            zg = _dot(h, w_in_ref[:, 2 * di + lo:2 * di + lo + gd])
            for c in range(tm // CHUNK):
                rows = slice(c * CHUNK, (c + 1) * CHUNK)
                mix = _dot(wg, vn_scr[rows, lo:lo + gd]) + bias
                gated_scr[rows, lo:lo + gd] = (u[rows] * mix * _silu(zg[rows])).astype(BF16)
        gated = gated_scr[...]
    y = _dot(gated, w_out_ref[...])
    o_ref[...] = _ln(alpha * x + gate * y, lng_ref[...], lnb_ref[...])


def _layer0(x, ada, w_in, sgu_g, sgu_b, w_s, b_s, w_out, ln_g, ln_b, *, alpha, decode):
    r, d = x.shape
    di = sgu_g.shape[-1]
    n_groups = w_s.shape[0]
    nb = ada.shape[0]
    if decode:
        tm = r
        ws_arg = jnp.repeat(w_s[:, 0, 0], di // n_groups)[None, :]
        bs_arg = jnp.repeat(b_s[:, 0], di // n_groups)[None, :]
        ws_spec = _const_spec((1, di))
        bs_spec = _const_spec((1, di))
        out_shape = (jax.ShapeDtypeStruct((r, d), F32), jax.ShapeDtypeStruct((r, di), F32))
        out_specs = (pl.BlockSpec((tm, d), lambda b, t: (b, 0)),
                     pl.BlockSpec((tm, di), lambda b, t: (b, 0)))
        scratch = []
        steps = 1
    else:
        tm = TM_DENSE
        ws_arg = w_s
        bs_arg = jnp.broadcast_to(b_s[:, :, None], (n_groups, CHUNK, LANES))
        ws_spec = _const_spec((n_groups, CHUNK, CHUNK))
        bs_spec = _const_spec((n_groups, CHUNK, LANES))
        out_shape = jax.ShapeDtypeStruct((r, d), F32)
        steps = r // nb // tm
        out_specs = pl.BlockSpec((tm, d), lambda b, t: (b * steps + t, 0))
        scratch = [pltpu.VMEM((tm, di), BF16), pltpu.VMEM((tm, di), BF16)]
    return pl.pallas_call(
        functools.partial(_layer0_kernel, alpha=alpha, decode=decode),
        grid=(nb, steps),
        in_specs=[
            pl.BlockSpec((tm, d), lambda b, t: (b * steps + t, 0)),
            pl.BlockSpec((1,) + ada.shape[1:], lambda b, t: (b, 0, 0)),
            _const_spec(w_in.shape),
            _const_spec((1, di)),
            _const_spec((1, di)),
            ws_spec,
            bs_spec,
            _const_spec(w_out.shape),
            _const_spec((1, d)),
            _const_spec((1, d)),
        ],
        out_specs=out_specs,
        out_shape=out_shape,
        scratch_shapes=scratch,
        compiler_params=pltpu.CompilerParams(
            dimension_semantics=("arbitrary", "arbitrary"), vmem_limit_bytes=VMEM_LIMIT),
        name="layer0_decode" if decode else "layer0",
    )(x, ada, w_in, sgu_g[None, :], sgu_b[None, :], ws_arg, bs_arg, w_out, ln_g[None, :], ln_b[None, :])


def _proj1_kernel(x_ref, ada_ref, akv_ref, w_kv_ref, w_in_ref,
                  k_ref, v_ref, kb_ref, vb_ref, q_ref, sz_ref, *, q_scale, keys_on_lanes):
    d = x_ref.shape[-1]
    aw = q_ref.shape[-1]
    x = x_ref[...]
    ada = ada_ref[0]
    akv = akv_ref[0]
    hk = (x * (1.0 + akv[:, d:2 * d]) + akv[:, 0:d]).astype(BF16)
    if keys_on_lanes:
        kv = _dot_nt(w_kv_ref[...], hk)
        k, v = kv[0:aw], kv[aw:2 * aw]
    else:
        kv = _dot(hk, w_kv_ref[...])
        k, v = kv[:, 0:aw], kv[:, aw:2 * aw]
    k_ref[...] = k
    v_ref[...] = v
    kb_ref[...] = k.astype(BF16)
    vb_ref[...] = v.astype(BF16)
    h = (x * (1.0 + ada[:, d:2 * d]) + ada[:, 0:d]).astype(BF16)
    qz = _dot(h, w_in_ref[...])
    q_ref[...] = (qz[:, 0:aw] * q_scale).astype(BF16)
    sz_ref[...] = _silu(qz[:, aw:2 * aw])


def _proj1(x, ada, akv, w_kv, w_in, *, tm, keys_on_lanes):
    r, d = x.shape
    aw = w_in.shape[1] // 2
    nb = ada.shape[0]
    steps = r // nb // tm
    row_spec = lambda n: pl.BlockSpec((tm, n), lambda b, t: (b * steps + t, 0))
    if keys_on_lanes:
        kv_spec = pl.BlockSpec((None, aw, tm), lambda b, t: (b, 0, t))
        kv_shape = (nb, aw, r // nb)
    else:
        kv_spec = row_spec(aw)
        kv_shape = (r, aw)
    return pl.pallas_call(
        functools.partial(_proj1_kernel, q_scale=HEAD_DIM ** -0.5 * LOG2E, keys_on_lanes=keys_on_lanes),
        grid=(nb, steps),
        in_specs=[
            row_spec(d),
            pl.BlockSpec((1,) + ada.shape[1:], lambda b, t: (b, 0, 0)),
            pl.BlockSpec((1,) + akv.shape[1:], lambda b, t: (b, 0, 0)),
            _const_spec(w_kv.shape),
            _const_spec(w_in.shape),
        ],
        out_specs=(kv_spec,) * 4 + (row_spec(aw),) * 2,
        out_shape=(
            jax.ShapeDtypeStruct(kv_shape, F32), jax.ShapeDtypeStruct(kv_shape, F32),
            jax.ShapeDtypeStruct(kv_shape, BF16), jax.ShapeDtypeStruct(kv_shape, BF16),
            jax.ShapeDtypeStruct((r, aw), BF16), jax.ShapeDtypeStruct((r, aw), F32),
        ),
        compiler_params=pltpu.CompilerParams(
            dimension_semantics=("arbitrary", "arbitrary"), vmem_limit_bytes=VMEM_LIMIT),
        name="proj1",
    )(x, ada, akv, w_kv, w_in)


def _sb_logs(z):
    sign = jnp.int32(-2 ** 31)
    e = jnp.exp2(lax.bitcast_convert_type(lax.bitcast_convert_type(z, jnp.int32) | sign, F32))
    l2 = jnp.log2(1.0 + e)
    lb = jnp.minimum(z, 0.0) - l2
    return lb, lb - z


def _split_bf16(x):
    hi = x.astype(BF16)
    lo = (x - hi.astype(F32)).astype(BF16)
    return hi, lo


def _sb_prompt_kernel(bias_ref, q_ref, k_ref, v_ref, o_ref, acc_ref):
    hb = pl.program_id(1)
    qi = pl.program_id(2)
    tq, width = q_ref.shape[1], q_ref.shape[2]
    n_heads = width // HEAD_DIM
    per_tile = LANES // HEAD_DIM
    lane = lax.broadcasted_iota(jnp.int32, (tq, LANES), 1)
    lane_head = lane // HEAD_DIM
    qs, biases = [], []
    for h in range(n_heads):
        lt = h // per_tile
        q_tile = q_ref[0, :, lt * LANES:(lt + 1) * LANES].astype(F32)
        qs.append(jnp.where(lane_head == h % per_tile, q_tile, 0.0).astype(BF16))
        biases.append(bias_ref[hb * n_heads + h] * LOG2E)
    r_i = lax.broadcasted_iota(jnp.int32, (TK, TK), 0)
    c_i = lax.broadcasted_iota(jnp.int32, (TK, TK), 1)
    later = jnp.where(r_i > c_i, 1.0, 0.0).astype(BF16)

    def tiles(start, carries, mask):
        heads = range(n_heads)
        rows = [slice((h // per_tile) * LANES, (h // per_tile + 1) * LANES) for h in heads]
        zs = [_dot(qs[h], k_ref[0, rows[h], pl.ds(start, TK)]) + biases[h] for h in heads]
        logs = [_sb_logs(z) for z in zs]
        l1s = [l1 if mask is None else jnp.where(mask, l1, 0.0) for _, l1 in logs]
        css = [_dot(l1.astype(BF16), later) for l1 in l1s]
        new_carries = []
        for h in heads:
            w = jnp.exp2(logs[h][0] + carries[h] + css[h])
            if mask is not None:
                w = jnp.where(mask, w, 0.0)
            acc_ref[h] += _dot_nt(w.astype(BF16), v_ref[0, rows[h], pl.ds(start, TK)])
            new_carries.append(carries[h] + jnp.sum(l1s[h], axis=1, keepdims=True))
        return new_carries

    acc_ref[...] = jnp.zeros_like(acc_ref)
    carries = [jnp.zeros((tq, 1), F32)] * n_heads
    q_pos = qi * tq + lax.broadcasted_iota(jnp.int32, (tq, TK), 0)
    k_off = lax.broadcasted_iota(jnp.int32, (tq, TK), 1)
    for dblk in reversed(range(tq // TK)):
        start = pl.multiple_of(qi * tq + dblk * TK, TK)
        carries = tiles(start, carries, (start + k_off) < q_pos)

    n_past = qi * (tq // TK)

    def body(i, carry):
        start = pl.multiple_of((n_past - 1 - i) * TK, TK)
        return tuple(tiles(start, list(carry), None))

    lax.fori_loop(0, n_past, body, tuple(carries))
    first = lane_head == 0
    o_ref[0] = jnp.concatenate(
        [jnp.where(first, acc_ref[2 * lt], acc_ref[2 * lt + 1]) for lt in range(width // LANES)], axis=1)


def _sb_prompt(q, kt, vt, bias):
    r, aw = q.shape
    bsz, _, t = kt.shape
    q3 = q.reshape(bsz, t, aw)
    out = pl.pallas_call(
        _sb_prompt_kernel,
        grid=(bsz, aw // HEAD_BLOCK, t // TQ),
        in_specs=[
            pl.BlockSpec(memory_space=pltpu.SMEM),
            pl.BlockSpec((1, TQ, HEAD_BLOCK), lambda b, hb, qi: (b, qi, hb)),
            pl.BlockSpec((1, HEAD_BLOCK, t), lambda b, hb, qi: (b, hb, 0)),
            pl.BlockSpec((1, HEAD_BLOCK, t), lambda b, hb, qi: (b, hb, 0)),
        ],
        out_specs=pl.BlockSpec((1, TQ, HEAD_BLOCK), lambda b, hb, qi: (b, qi, hb)),
        scratch_shapes=[pltpu.VMEM((HEAD_BLOCK // HEAD_DIM, TQ, LANES), F32)],
        out_shape=jax.ShapeDtypeStruct((bsz, t, aw), F32),
        compiler_params=pltpu.CompilerParams(
            dimension_semantics=("arbitrary", "arbitrary", "arbitrary"), vmem_limit_bytes=VMEM_LIMIT),
        name="sb_prompt",
    )(bias, q3, kt, vt)
    return out.reshape(r, aw)


def _sb_decode_kernel(pt_ref, q_ref, bias_ref, kn_ref, vn_ref, *rest, past_len):
    del pt_ref
    n = PAGES_PER_STEP
    k_refs, v_refs = rest[:n], rest[n:2 * n]
    o_ref, carry_ref, acc_ref = rest[2 * n:]
    g = pl.program_id(1)
    n_heads, hd, page = k_refs[0].shape
    aw = n_heads * hd
    own = (lax.broadcasted_iota(jnp.int32, (n_heads, aw), 1) // hd
           == lax.broadcasted_iota(jnp.int32, (n_heads, aw), 0))
    q_rows = jnp.where(own, jnp.broadcast_to(q_ref[0].astype(F32), (n_heads, aw)), 0.0)
    bias = bias_ref[...] * LOG2E
    t_a = lax.broadcasted_iota(jnp.int32, (page, page), 0)
    t_b = lax.broadcasted_iota(jnp.int32, (page, page), 1)
    later = jnp.where(t_a > t_b, 1.0, 0.0).astype(BF16)
    ones = jnp.ones((page, page), BF16)

    @pl.when(g == 0)
    def _():
        z = jnp.sum(q_rows * kn_ref[0], axis=1, keepdims=True) + bias
        lb, l1 = _sb_logs(z)
        key = lax.broadcasted_iota(jnp.int32, (n_heads, page), 1)
        mask = ((past_len + key) < past_len) & (key < 1)
        l1 = jnp.where(mask, l1, 0.0)
        w = jnp.where(mask, jnp.exp2(lb), 0.0)
        carry_ref[...] = jnp.broadcast_to(jnp.sum(l1, axis=1, keepdims=True), (n_heads, page))
        acc_ref[...] = jnp.sum(w, axis=1, keepdims=True) * vn_ref[0]

    k_cat = jnp.concatenate([r[...].reshape(aw, page).astype(BF16) for r in reversed(k_refs)], axis=1)
    v_cat = jnp.concatenate([r[...].reshape(aw, page).astype(BF16) for r in reversed(v_refs)], axis=1)
    z_cat = _dot(q_rows.astype(BF16), k_cat)
    z = jnp.concatenate([z_cat[:, p * page:(p + 1) * page] + bias for p in range(n)], axis=0)
    n_rows = n * n_heads
    lb, l1 = _sb_logs(z)
    hl = jnp.concatenate(_split_bf16(l1), axis=0)
    within = _dot(hl, later)
    within = within[0:n_rows] + within[n_rows:2 * n_rows]
    totals = _dot(hl, ones)
    totals = totals[0:n_rows] + totals[n_rows:2 * n_rows]
    r_a = lax.broadcasted_iota(jnp.int32, (n_rows, 2 * n_rows), 0)
    r_b = lax.broadcasted_iota(jnp.int32, (n_rows, 2 * n_rows), 1) % n_rows
    newer_pages = jnp.where((r_b % n_heads == r_a % n_heads) & (r_b // n_heads > r_a // n_heads), 1.0, 0.0)
    newer = _dot(newer_pages.astype(BF16), jnp.concatenate(_split_bf16(totals), axis=0))
    carry = carry_ref[...]
    w = jnp.exp2(lb + within + newer + jnp.concatenate([carry] * n, axis=0))
    w_cat = jnp.concatenate([w[p * n_heads:(p + 1) * n_heads] for p in range(n)], axis=1).astype(BF16)
    acc_ref[...] += _dot_nt(w_cat, v_cat)
    carry_ref[...] = carry + newer[0:n_heads] + totals[0:n_heads]

    @pl.when(g == pl.num_programs(1) - 1)
    def _():
        o_ref[0] = jnp.sum(jnp.where(own, acc_ref[...], 0.0), axis=0, keepdims=True)


def _sb_decode(q, k_new, v_new, bias, cache_k, cache_v, page_table):
    s, aw = q.shape
    n_pool, page, n_heads, hd = cache_k.shape
    n_pages = page_table.shape[1]
    n = PAGES_PER_STEP
    ck = jnp.transpose(cache_k, (0, 2, 3, 1))
    cv = jnp.transpose(cache_v, (0, 2, 3, 1))
    bias_rows = jnp.broadcast_to(bias[:, None], (n_heads, page))

    def page_spec(i):
        return pl.BlockSpec(
            (None, n_heads, hd, page),
            lambda b, g, pt: (pt[b * n_pages + n_pages - 1 - (g * n + i)], 0, 0, 0))

    row_spec = pl.BlockSpec((1, 1, aw), lambda b, g, pt: (b, 0, 0))
    out = pl.pallas_call(
        functools.partial(_sb_decode_kernel, past_len=n_pages * page),
        grid_spec=pltpu.PrefetchScalarGridSpec(
            num_scalar_prefetch=1,
            grid=(s, n_pages // n),
            in_specs=[row_spec, pl.BlockSpec((n_heads, page), lambda b, g, pt: (0, 0)), row_spec, row_spec]
            + [page_spec(i) for i in range(n)] * 2,
            out_specs=row_spec,
            scratch_shapes=[pltpu.VMEM((n_heads, page), F32), pltpu.VMEM((n_heads, aw), F32)],
        ),
        out_shape=jax.ShapeDtypeStruct((s, 1, aw), F32),
        compiler_params=pltpu.CompilerParams(
            dimension_semantics=("arbitrary", "arbitrary"), vmem_limit_bytes=VMEM_LIMIT),
        name="sb_decode",
    )(page_table.reshape(-1), q.reshape(s, 1, aw), bias_rows, k_new.reshape(s, 1, aw),
      v_new.reshape(s, 1, aw), *([ck] * n), *([cv] * n))
    return out.reshape(s, aw)


def _out1_kernel(o_ref, sz_ref, x_ref, ada_ref, w_ref, lng_ref, lnb_ref, y_ref, *, alpha):
    d = x_ref.shape[-1]
    gated = (o_ref[...] * sz_ref[...]).astype(BF16)
    y = _dot(gated, w_ref[...])
    gate = ada_ref[0][:, 2 * d:3 * d]
    y_ref[...] = _ln(alpha * x_ref[...] + gate * y, lng_ref[...], lnb_ref[...])


def _out1(o, sz, x, ada, w_out, ln_g, ln_b, *, alpha, tm):
    r, d = x.shape
    aw = o.shape[1]
    nb = ada.shape[0]
    steps = r // nb // tm
    row_spec = lambda n: pl.BlockSpec((tm, n), lambda b, t: (b * steps + t, 0))
    return pl.pallas_call(
        functools.partial(_out1_kernel, alpha=alpha),
        grid=(nb, steps),
        in_specs=[
            row_spec(aw), row_spec(aw), row_spec(d),
            pl.BlockSpec((1,) + ada.shape[1:], lambda b, t: (b, 0, 0)),
            _const_spec(w_out.shape), _const_spec((1, d)), _const_spec((1, d)),
        ],
        out_specs=row_spec(d),
        out_shape=jax.ShapeDtypeStruct((r, d), F32),
        compiler_params=pltpu.CompilerParams(
            dimension_semantics=("arbitrary", "arbitrary"), vmem_limit_bytes=VMEM_LIMIT),
        name="out1",
    )(o, sz, x, ada, w_out, ln_g[None, :], ln_b[None, :])


def kernel(x_prompt, x_sample, cache_k, cache_v, page_table, c_prompt, c_sample, w_ada, b_ada, ln_g, ln_b, w_in_a, sgu_g, sgu_b, w_s, b_s, w_out_a, w_ada_kv, b_ada_kv, w_kv, w_in_b, w_out_b, sb_bias):
    bsz, seq, d = x_prompt.shape
    dec_b, dec_seq, _ = x_sample.shape
    depth = w_ada.shape[0]
    assert depth == 2 and w_in_a.shape[0] == 1 and w_in_b.shape[0] == 1
    assert dec_seq == 1 and seq % TM_DENSE == 0 and seq % TQ == 0 and TQ % TK == 0
    assert page_table.shape[1] % PAGES_PER_STEP == 0
    n_heads = sb_bias.shape[1]
    alpha = (2 * depth) ** 0.25

    c_all = jnp.concatenate([c_prompt, c_sample], axis=0)
    ada = _ada(c_all, w_ada, b_ada)
    akv = _ada(c_all, w_ada_kv[None], b_ada_kv[None])[0]
    split = lambda a: (a[:bsz, None, :], a[None, bsz:, :])
    ada0_p, ada0_s = split(ada[0])
    ada1_p, ada1_s = split(ada[1])
    akv_p, akv_s = split(akv)

    w_in_a_b = w_in_a[0].astype(BF16)
    w_out_a_b = w_out_a[0].astype(BF16)
    w_kv_b = w_kv.astype(BF16)
    w_in_b_b = w_in_b[0].astype(BF16)
    w_out_b_b = w_out_b[0].astype(BF16)
    l0 = functools.partial(_layer0, w_in=w_in_a_b, sgu_g=sgu_g[0], sgu_b=sgu_b[0], w_s=w_s[0], b_s=b_s[0],
                           w_out=w_out_a_b, ln_g=ln_g[0], ln_b=ln_b[0], alpha=alpha)

    xp = x_prompt.reshape(bsz * seq, d)
    x1p = l0(xp, ada0_p, decode=False)
    ktp, vtp, ktb, vtb, qp, szp = _proj1(x1p, ada1_p, akv_p, w_kv_b.T, w_in_b_b, tm=TM_DENSE, keys_on_lanes=True)
    op = _sb_prompt(qp, ktb, vtb, sb_bias[0])
    yp = _out1(op, szp, x1p, ada1_p, w_out_b_b, ln_g[1], ln_b[1], alpha=alpha, tm=TM_DENSE)

    xs = x_sample.reshape(dec_b * dec_seq, d)
    x1s, vns = l0(xs, ada0_s, decode=True)
    ks, vs, _, _, qs, szs = _proj1(x1s, ada1_s, akv_s, w_kv_b, w_in_b_b, tm=dec_b, keys_on_lanes=False)
    osamp = _sb_decode(qs, ks, vs, sb_bias[0], cache_k, cache_v, page_table)
    ys = _out1(osamp, szs, x1s, ada1_s, w_out_b_b, ln_g[1], ln_b[1], alpha=alpha, tm=dec_b)

    hd = (n_heads, d // n_heads)
    to_rows = lambda a: jnp.transpose(a.reshape(bsz, *hd, seq), (0, 3, 1, 2))
    return (yp.reshape(bsz, seq, d), ys.reshape(dec_b, dec_seq, d),
            to_rows(ktp), to_rows(vtp),
            ks.reshape(dec_b, dec_seq, *hd), vs.reshape(dec_b, dec_seq, *hd),
            vns.reshape(1, dec_b, dec_seq, -1))
```

```python
import functools
import math

import jax
import jax.numpy as jnp
from jax import lax
from jax.experimental import pallas as pl
from jax.experimental.pallas import tpu as pltpu

F32 = jnp.float32
BF16 = jnp.bfloat16

HEAD_DIM = 64
CHUNK = 128
LN_EPS = 1e-5
LOG2E = math.log2(math.e)
LANES = 128
VMEM_LIMIT = 56 * 1024 * 1024

TM_DENSE = 512
TQ = 256
TK = 256
HEAD_BLOCK = 512
PAGES_PER_STEP = 16


def _silu(x):
    return x * (1.0 / (1.0 + jnp.exp(-x)))


def _ln(x, g, b):
    mu = jnp.mean(x, axis=-1, keepdims=True)
    xc = x - mu
    var = jnp.mean(xc * xc, axis=-1, keepdims=True)
    return xc * lax.rsqrt(var + LN_EPS) * g + b


def _dot(a, b):
    return jnp.dot(a, b, preferred_element_type=F32)


def _dot_nt(a, b):
    return lax.dot_general(a, b, (((1,), (1,)), ((), ())), preferred_element_type=F32)


def _dot_tn(a, b):
    return lax.dot_general(a, b, (((0,), (0,)), ((), ())), preferred_element_type=F32)


def _const_spec(shape):
    n = len(shape)
    return pl.BlockSpec(shape, lambda *_: (0,) * n, pipeline_mode=pl.Buffered(1))


def _ada_kernel(c_ref, w_ref, b_ref, o_ref):
    sc = _silu(c_ref[...]).astype(BF16)
    o_ref[...] = _dot(sc, w_ref[...].astype(BF16)) + b_ref[...]


def _ada(c, w, b):
    n_layers, d, n = w.shape
    r = c.shape[0]
    bn = 1024
    return pl.pallas_call(
        _ada_kernel,
        grid=(n_layers, n // bn),
        in_specs=[
            pl.BlockSpec((r, d), lambda l, j: (0, 0)),
            pl.BlockSpec((None, d, bn), lambda l, j: (l, 0, j)),
            pl.BlockSpec((None, 1, bn), lambda l, j: (l, 0, j)),
        ],
        out_specs=pl.BlockSpec((None, r, bn), lambda l, j: (l, 0, j)),
        out_shape=jax.ShapeDtypeStruct((n_layers, r, n), F32),
        name="ada",
    )(c, w, b.reshape(n_layers, 1, n))


def _layer0_kernel(x_ref, ada_ref, w_in_ref, sg_ref, sb_ref, ws_ref, bs_ref, w_out_ref,
                   lng_ref, lnb_ref, *rest, alpha, decode):
    d = x_ref.shape[-1]
    di = sg_ref.shape[-1]
    n_groups = ws_ref.shape[0]
    gd = di // n_groups
    x = x_ref[...]
    ada = ada_ref[0]
    shift, scale, gate = ada[:, 0:d], ada[:, d:2 * d], ada[:, 2 * d:3 * d]
    h = (x * (1.0 + scale) + shift).astype(BF16)
    vn = _ln(_dot(h, w_in_ref[:, di:2 * di]), sg_ref[...], sb_ref[...])
    if decode:
        o_ref, vn_ref = rest
        vn_ref[...] = vn
        mix = vn * ws_ref[...] + bs_ref[...]
        u = _dot(h, w_in_ref[:, 0:di])
        zg = _dot(h, w_in_ref[:, 2 * di:3 * di])
        gated = (u * mix * _silu(zg)).astype(BF16)
    else:
        o_ref, vn_scr, gated_scr = rest
        tm = x.shape[0]
        vn_scr[...] = vn.astype(BF16)
        row = lax.broadcasted_iota(jnp.int32, (CHUNK, CHUNK), 0)
        col = lax.broadcasted_iota(jnp.int32, (CHUNK, CHUNK), 1)
        causal = col <= row
        for g in range(n_groups):
            lo = g * gd
            wg = jnp.where(causal, ws_ref[g], 0.0).astype(BF16)
            bias = jnp.concatenate([bs_ref[g]] * (gd // LANES), axis=1)
            u = _dot(h, w_in_ref[:, lo:lo + gd])
            zg = _dot(h, w_in_ref[:, 2 * di + lo:2 * di + lo + gd])
            for c in range(tm // CHUNK):
                rows = slice(c * CHUNK, (c + 1) * CHUNK)
                mix = _dot(wg, vn_scr[rows, lo:lo + gd]) + bias
                gated_scr[rows, lo:lo + gd] = (u[rows] * mix * _silu(zg[rows])).astype(BF16)
        gated = gated_scr[...]
    y = _dot(gated, w_out_ref[...])
    o_ref[...] = _ln(alpha * x + gate * y, lng_ref[...], lnb_ref[...])


def _layer0(x, ada, w_in, sgu_g, sgu_b, w_s, b_s, w_out, ln_g, ln_b, *, alpha, decode):
    r, d = x.shape
    di = sgu_g.shape[-1]
    n_groups = w_s.shape[0]
    nb = ada.shape[0]
    if decode:
        tm = r
        ws_arg = jnp.repeat(w_s[:, 0, 0], di // n_groups)[None, :]
        bs_arg = jnp.repeat(b_s[:, 0], di // n_groups)[None, :]
        ws_spec = _const_spec((1, di))
        bs_spec = _const_spec((1, di))
        out_shape = (jax.ShapeDtypeStruct((r, d), F32), jax.ShapeDtypeStruct((r, di), F32))
        out_specs = (pl.BlockSpec((tm, d), lambda b, t: (b, 0)),
                     pl.BlockSpec((tm, di), lambda b, t: (b, 0)))
        scratch = []
        steps = 1
    else:
        tm = TM_DENSE
        ws_arg = w_s
        bs_arg = jnp.broadcast_to(b_s[:, :, None], (n_groups, CHUNK, LANES))
        ws_spec = _const_spec((n_groups, CHUNK, CHUNK))
        bs_spec = _const_spec((n_groups, CHUNK, LANES))
        out_shape = jax.ShapeDtypeStruct((r, d), F32)
        steps = r // nb // tm
        out_specs = pl.BlockSpec((tm, d), lambda b, t: (b * steps + t, 0))
        scratch = [pltpu.VMEM((tm, di), BF16), pltpu.VMEM((tm, di), BF16)]
    return pl.pallas_call(
        functools.partial(_layer0_kernel, alpha=alpha, decode=decode),
        grid=(nb, steps),
        in_specs=[
            pl.BlockSpec((tm, d), lambda b, t: (b * steps + t, 0)),
            pl.BlockSpec((1,) + ada.shape[1:], lambda b, t: (b, 0, 0)),
            _const_spec(w_in.shape),
            _const_spec((1, di)),
            _const_spec((1, di)),
            ws_spec,
            bs_spec,
            _const_spec(w_out.shape),
            _const_spec((1, d)),
            _const_spec((1, d)),
        ],
        out_specs=out_specs,
        out_shape=out_shape,
        scratch_shapes=scratch,
        compiler_params=pltpu.CompilerParams(
            dimension_semantics=("arbitrary", "arbitrary"), vmem_limit_bytes=VMEM_LIMIT),
        name="layer0_decode" if decode else "layer0",
    )(x, ada, w_in, sgu_g[None, :], sgu_b[None, :], ws_arg, bs_arg, w_out, ln_g[None, :], ln_b[None, :])


def _proj1_kernel(x_ref, ada_ref, akv_ref, w_kv_ref, w_in_ref,
                  k_ref, v_ref, kb_ref, vb_ref, q_ref, sz_ref, *, q_scale, keys_on_lanes):
    d = x_ref.shape[-1]
    aw = q_ref.shape[-1]
    x = x_ref[...]
    ada = ada_ref[0]
    akv = akv_ref[0]
    hk = (x * (1.0 + akv[:, d:2 * d]) + akv[:, 0:d]).astype(BF16)
    if keys_on_lanes:
        kv = _dot_nt(w_kv_ref[...], hk)
        k, v = kv[0:aw], kv[aw:2 * aw]
    else:
        kv = _dot(hk, w_kv_ref[...])
        k, v = kv[:, 0:aw], kv[:, aw:2 * aw]
    k_ref[...] = k
    v_ref[...] = v
    kb_ref[...] = k.astype(BF16)
    vb_ref[...] = v.astype(BF16)
    h = (x * (1.0 + ada[:, d:2 * d]) + ada[:, 0:d]).astype(BF16)
    qz = _dot(h, w_in_ref[...])
    q_ref[...] = (qz[:, 0:aw] * q_scale).astype(BF16)
    sz_ref[...] = _silu(qz[:, aw:2 * aw])


def _proj1(x, ada, akv, w_kv, w_in, *, tm, keys_on_lanes):
    r, d = x.shape
    aw = w_in.shape[1] // 2
    nb = ada.shape[0]
    steps = r // nb // tm
    row_spec = lambda n: pl.BlockSpec((tm, n), lambda b, t: (b * steps + t, 0))
    if keys_on_lanes:
        kv_spec = pl.BlockSpec((None, aw, tm), lambda b, t: (b, 0, t))
        kv_shape = (nb, aw, r // nb)
    else:
        kv_spec = row_spec(aw)
        kv_shape = (r, aw)
    return pl.pallas_call(
        functools.partial(_proj1_kernel, q_scale=HEAD_DIM ** -0.5 * LOG2E, keys_on_lanes=keys_on_lanes),
        grid=(nb, steps),
        in_specs=[
            row_spec(d),
            pl.BlockSpec((1,) + ada.shape[1:], lambda b, t: (b, 0, 0)),
            pl.BlockSpec((1,) + akv.shape[1:], lambda b, t: (b, 0, 0)),
            _const_spec(w_kv.shape),
            _const_spec(w_in.shape),
        ],
        out_specs=(kv_spec,) * 4 + (row_spec(aw),) * 2,
        out_shape=(
            jax.ShapeDtypeStruct(kv_shape, F32), jax.ShapeDtypeStruct(kv_shape, F32),
            jax.ShapeDtypeStruct(kv_shape, BF16), jax.ShapeDtypeStruct(kv_shape, BF16),
            jax.ShapeDtypeStruct((r, aw), BF16), jax.ShapeDtypeStruct((r, aw), F32),
        ),
        compiler_params=pltpu.CompilerParams(
            dimension_semantics=("arbitrary", "arbitrary"), vmem_limit_bytes=VMEM_LIMIT),
        name="proj1",
    )(x, ada, akv, w_kv, w_in)


def _sb_logs(z):
    sign = jnp.int32(-2 ** 31)
    e = jnp.exp2(lax.bitcast_convert_type(lax.bitcast_convert_type(z, jnp.int32) | sign, F32))
    l2 = jnp.log2(1.0 + e)
    lb = jnp.minimum(z, 0.0) - l2
    return lb, lb - z


def _split_bf16(x):
    hi = x.astype(BF16)
    lo = (x - hi.astype(F32)).astype(BF16)
    return hi, lo


def _sb_prompt_kernel(bias_ref, q_ref, k_ref, v_ref, o_ref, acc_ref, z_ref):
    hb = pl.program_id(1)
    qi = pl.program_id(2)
    tq, width = q_ref.shape[1], q_ref.shape[2]
    n_heads = width // HEAD_DIM
    per_tile = LANES // HEAD_DIM
    lane = lax.broadcasted_iota(jnp.int32, (tq, LANES), 1)
    lane_head = lane // HEAD_DIM
    qs, biases = [], []
    for h in range(n_heads):
        lt = h // per_tile
        q_tile = q_ref[0, :, lt * LANES:(lt + 1) * LANES].astype(F32)
        qs.append(jnp.where(lane_head == h % per_tile, q_tile, 0.0).astype(BF16))
        biases.append(bias_ref[hb * n_heads + h] * LOG2E)
    r_i = lax.broadcasted_iota(jnp.int32, (TK, TK), 0)
    c_i = lax.broadcasted_iota(jnp.int32, (TK, TK), 1)
    later = jnp.where(r_i > c_i, 1.0, 0.0).astype(BF16)

    heads = range(n_heads)
    rows = [slice((h // per_tile) * LANES, (h // per_tile + 1) * LANES) for h in heads]

    def scores(start):
        return [_dot(qs[h], k_ref[0, rows[h], pl.ds(start, TK)]) + biases[h] for h in heads]

    def tiles(start, carries, mask, zs, prefetch=None):
        logs = [_sb_logs(z) for z in zs]
        l1s = [l1 if mask is None else jnp.where(mask, l1, 0.0) for _, l1 in logs]
        css = [_dot(l1.astype(BF16), later) for l1 in l1s]
        if prefetch is not None:
            for h, z in enumerate(scores(prefetch)):
                z_ref[h] = z
        new_carries = []
        for h in heads:
            w = jnp.exp2(logs[h][0] + carries[h] + css[h])
            if mask is not None:
                w = jnp.where(mask, w, 0.0)
            acc_ref[h] += _dot_nt(w.astype(BF16), v_ref[0, rows[h], pl.ds(start, TK)])
            new_carries.append(carries[h] + jnp.sum(l1s[h], axis=1, keepdims=True))
        return new_carries

    acc_ref[...] = jnp.zeros_like(acc_ref)
    carries = [jnp.zeros((tq, 1), F32)] * n_heads
    q_pos = qi * tq + lax.broadcasted_iota(jnp.int32, (tq, TK), 0)
    k_off = lax.broadcasted_iota(jnp.int32, (tq, TK), 1)
    n_past = qi * (tq // TK)
    past_start = lambda j: pl.multiple_of(jnp.maximum(n_past - 1 - j, 0) * TK, TK)
    for dblk in reversed(range(tq // TK)):
        start = pl.multiple_of(qi * tq + dblk * TK, TK)
        carries = tiles(start, carries, (start + k_off) < q_pos, scores(start),
                        prefetch=past_start(0) if dblk == 0 else None)

    def body(i, carry):
        return tuple(tiles(past_start(i), list(carry), None, [z_ref[h] for h in heads], prefetch=past_start(i + 1)))

    lax.fori_loop(0, n_past, body, tuple(carries))
    first = lane_head == 0
    o_ref[0] = jnp.concatenate(
        [jnp.where(first, acc_ref[2 * lt], acc_ref[2 * lt + 1]) for lt in range(width // LANES)], axis=1)


def _sb_prompt(q, kt, vt, bias):
    r, aw = q.shape
    bsz, _, t = kt.shape
    q3 = q.reshape(bsz, t, aw)
    out = pl.pallas_call(
        _sb_prompt_kernel,
        grid=(bsz, aw // HEAD_BLOCK, t // TQ),
        in_specs=[
            pl.BlockSpec(memory_space=pltpu.SMEM),
            pl.BlockSpec((1, TQ, HEAD_BLOCK), lambda b, hb, qi: (b, qi, hb)),
            pl.BlockSpec((1, HEAD_BLOCK, t), lambda b, hb, qi: (b, hb, 0)),
            pl.BlockSpec((1, HEAD_BLOCK, t), lambda b, hb, qi: (b, hb, 0)),
        ],
        out_specs=pl.BlockSpec((1, TQ, HEAD_BLOCK), lambda b, hb, qi: (b, qi, hb)),
        scratch_shapes=[pltpu.VMEM((HEAD_BLOCK // HEAD_DIM, TQ, LANES), F32),
                        pltpu.VMEM((HEAD_BLOCK // HEAD_DIM, TQ, TK), F32)],
        out_shape=jax.ShapeDtypeStruct((bsz, t, aw), F32),
        compiler_params=pltpu.CompilerParams(
            dimension_semantics=("arbitrary", "arbitrary", "arbitrary"), vmem_limit_bytes=VMEM_LIMIT),
        name="sb_prompt",
    )(bias, q3, kt, vt)
    return out.reshape(r, aw)


def _sb_decode_kernel(pt_ref, q_ref, bias_ref, kn_ref, vn_ref, *rest, past_len):
    del pt_ref
    n = PAGES_PER_STEP
    k_refs, v_refs = rest[:n], rest[n:2 * n]
    o_ref, carry_ref, acc_ref = rest[2 * n:]
    g = pl.program_id(1)
    n_heads, hd, page = k_refs[0].shape
    aw = n_heads * hd
    own = (lax.broadcasted_iota(jnp.int32, (n_heads, aw), 1) // hd
           == lax.broadcasted_iota(jnp.int32, (n_heads, aw), 0))
    q_rows = jnp.where(own, jnp.broadcast_to(q_ref[0].astype(F32), (n_heads, aw)), 0.0)
    bias = bias_ref[...] * LOG2E
    t_a = lax.broadcasted_iota(jnp.int32, (page, page), 0)
    t_b = lax.broadcasted_iota(jnp.int32, (page, page), 1)
    later = jnp.where(t_a > t_b, 1.0, 0.0).astype(BF16)
    ones = jnp.ones((page, page), BF16)

    @pl.when(g == 0)
    def _():
        z = jnp.sum(q_rows * kn_ref[0], axis=1, keepdims=True) + bias
        lb, l1 = _sb_logs(z)
        key = lax.broadcasted_iota(jnp.int32, (n_heads, page), 1)
        mask = ((past_len + key) < past_len) & (key < 1)
        l1 = jnp.where(mask, l1, 0.0)
        w = jnp.where(mask, jnp.exp2(lb), 0.0)
        carry_ref[...] = jnp.broadcast_to(jnp.sum(l1, axis=1, keepdims=True), (n_heads, page))
        acc_ref[...] = jnp.sum(w, axis=1, keepdims=True) * vn_ref[0]

    k_cat = jnp.concatenate([r[...].reshape(aw, page).astype(BF16) for r in reversed(k_refs)], axis=1)
    v_cat = jnp.concatenate([r[...].reshape(aw, page).astype(BF16) for r in reversed(v_refs)], axis=1)
    z_cat = _dot(q_rows.astype(BF16), k_cat)
    z = jnp.concatenate([z_cat[:, p * page:(p + 1) * page] + bias for p in range(n)], axis=0)
    n_rows = n * n_heads
    lb, l1 = _sb_logs(z)
    hl = jnp.concatenate(_split_bf16(l1), axis=0)
    within = _dot(hl, later)
    within = within[0:n_rows] + within[n_rows:2 * n_rows]
    totals = _dot(hl, ones)
    totals = totals[0:n_rows] + totals[n_rows:2 * n_rows]
    r_a = lax.broadcasted_iota(jnp.int32, (n_rows, 2 * n_rows), 0)
    r_b = lax.broadcasted_iota(jnp.int32, (n_rows, 2 * n_rows), 1) % n_rows
    newer_pages = jnp.where((r_b % n_heads == r_a % n_heads) & (r_b // n_heads > r_a // n_heads), 1.0, 0.0)
    newer = _dot(newer_pages.astype(BF16), jnp.concatenate(_split_bf16(totals), axis=0))
    carry = carry_ref[...]
    w = jnp.exp2(lb + within + newer + jnp.concatenate([carry] * n, axis=0))
    w_cat = jnp.concatenate([w[p * n_heads:(p + 1) * n_heads] for p in range(n)], axis=1).astype(BF16)
    acc_ref[...] += _dot_nt(w_cat, v_cat)
    carry_ref[...] = carry + newer[0:n_heads] + totals[0:n_heads]

    @pl.when(g == pl.num_programs(1) - 1)
    def _():
        o_ref[0] = jnp.sum(jnp.where(own, acc_ref[...], 0.0), axis=0, keepdims=True)


def _sb_decode(q, k_new, v_new, bias, cache_k, cache_v, page_table):
    s, aw = q.shape
    n_pool, page, n_heads, hd = cache_k.shape
    n_pages = page_table.shape[1]
    n = PAGES_PER_STEP
    ck = jnp.transpose(cache_k, (0, 2, 3, 1))
    cv = jnp.transpose(cache_v, (0, 2, 3, 1))
    bias_rows = jnp.broadcast_to(bias[:, None], (n_heads, page))

    def page_spec(i):
        return pl.BlockSpec(
            (None, n_heads, hd, page),
            lambda b, g, pt: (pt[b * n_pages + n_pages - 1 - (g * n + i)], 0, 0, 0))

    row_spec = pl.BlockSpec((1, 1, aw), lambda b, g, pt: (b, 0, 0))
    out = pl.pallas_call(
        functools.partial(_sb_decode_kernel, past_len=n_pages * page),
        grid_spec=pltpu.PrefetchScalarGridSpec(
            num_scalar_prefetch=1,
            grid=(s, n_pages // n),
            in_specs=[row_spec, pl.BlockSpec((n_heads, page), lambda b, g, pt: (0, 0)), row_spec, row_spec]
            + [page_spec(i) for i in range(n)] * 2,
            out_specs=row_spec,
            scratch_shapes=[pltpu.VMEM((n_heads, page), F32), pltpu.VMEM((n_heads, aw), F32)],
        ),
        out_shape=jax.ShapeDtypeStruct((s, 1, aw), F32),
        compiler_params=pltpu.CompilerParams(
            dimension_semantics=("arbitrary", "arbitrary"), vmem_limit_bytes=VMEM_LIMIT),
        name="sb_decode",
    )(page_table.reshape(-1), q.reshape(s, 1, aw), bias_rows, k_new.reshape(s, 1, aw),
      v_new.reshape(s, 1, aw), *([ck] * n), *([cv] * n))
    return out.reshape(s, aw)


def _out1_kernel(o_ref, sz_ref, x_ref, ada_ref, w_ref, lng_ref, lnb_ref, y_ref, *, alpha):
    d = x_ref.shape[-1]
    gated = (o_ref[...] * sz_ref[...]).astype(BF16)
    y = _dot(gated, w_ref[...])
    gate = ada_ref[0][:, 2 * d:3 * d]
    y_ref[...] = _ln(alpha * x_ref[...] + gate * y, lng_ref[...], lnb_ref[...])


def _out1(o, sz, x, ada, w_out, ln_g, ln_b, *, alpha, tm):
    r, d = x.shape
    aw = o.shape[1]
    nb = ada.shape[0]
    steps = r // nb // tm
    row_spec = lambda n: pl.BlockSpec((tm, n), lambda b, t: (b * steps + t, 0))
    return pl.pallas_call(
        functools.partial(_out1_kernel, alpha=alpha),
        grid=(nb, steps),
        in_specs=[
            row_spec(aw), row_spec(aw), row_spec(d),
            pl.BlockSpec((1,) + ada.shape[1:], lambda b, t: (b, 0, 0)),
            _const_spec(w_out.shape), _const_spec((1, d)), _const_spec((1, d)),
        ],
        out_specs=row_spec(d),
        out_shape=jax.ShapeDtypeStruct((r, d), F32),
        compiler_params=pltpu.CompilerParams(
            dimension_semantics=("arbitrary", "arbitrary"), vmem_limit_bytes=VMEM_LIMIT),
        name="out1",
    )(o, sz, x, ada, w_out, ln_g[None, :], ln_b[None, :])


def kernel(x_prompt, x_sample, cache_k, cache_v, page_table, c_prompt, c_sample, w_ada, b_ada, ln_g, ln_b, w_in_a, sgu_g, sgu_b, w_s, b_s, w_out_a, w_ada_kv, b_ada_kv, w_kv, w_in_b, w_out_b, sb_bias):
    bsz, seq, d = x_prompt.shape
    dec_b, dec_seq, _ = x_sample.shape
    depth = w_ada.shape[0]
    assert depth == 2 and w_in_a.shape[0] == 1 and w_in_b.shape[0] == 1
    assert dec_seq == 1 and seq % TM_DENSE == 0 and seq % TQ == 0 and TQ % TK == 0
    assert page_table.shape[1] % PAGES_PER_STEP == 0
    n_heads = sb_bias.shape[1]
    alpha = (2 * depth) ** 0.25

    c_all = jnp.concatenate([c_prompt, c_sample], axis=0)
    ada = _ada(c_all, w_ada, b_ada)
    akv = _ada(c_all, w_ada_kv[None], b_ada_kv[None])[0]
    split = lambda a: (a[:bsz, None, :], a[None, bsz:, :])
    ada0_p, ada0_s = split(ada[0])
    ada1_p, ada1_s = split(ada[1])
    akv_p, akv_s = split(akv)

    w_in_a_b = w_in_a[0].astype(BF16)
    w_out_a_b = w_out_a[0].astype(BF16)
    w_kv_b = w_kv.astype(BF16)
    w_in_b_b = w_in_b[0].astype(BF16)
    w_out_b_b = w_out_b[0].astype(BF16)
    l0 = functools.partial(_layer0, w_in=w_in_a_b, sgu_g=sgu_g[0], sgu_b=sgu_b[0], w_s=w_s[0], b_s=b_s[0],
                           w_out=w_out_a_b, ln_g=ln_g[0], ln_b=ln_b[0], alpha=alpha)

    xp = x_prompt.reshape(bsz * seq, d)
    x1p = l0(xp, ada0_p, decode=False)
    ktp, vtp, ktb, vtb, qp, szp = _proj1(x1p, ada1_p, akv_p, w_kv_b.T, w_in_b_b, tm=TM_DENSE, keys_on_lanes=True)
    op = _sb_prompt(qp, ktb, vtb, sb_bias[0])
    yp = _out1(op, szp, x1p, ada1_p, w_out_b_b, ln_g[1], ln_b[1], alpha=alpha, tm=TM_DENSE)

    xs = x_sample.reshape(dec_b * dec_seq, d)
    x1s, vns = l0(xs, ada0_s, decode=True)
    ks, vs, _, _, qs, szs = _proj1(x1s, ada1_s, akv_s, w_kv_b, w_in_b_b, tm=dec_b, keys_on_lanes=False)
    osamp = _sb_decode(qs, ks, vs, sb_bias[0], cache_k, cache_v, page_table)
    ys = _out1(osamp, szs, x1s, ada1_s, w_out_b_b, ln_g[1], ln_b[1], alpha=alpha, tm=dec_b)

    hd = (n_heads, d // n_heads)
    to_rows = lambda a: jnp.transpose(a.reshape(bsz, *hd, seq), (0, 3, 1, 2))
    return (yp.reshape(bsz, seq, d), ys.reshape(dec_b, dec_seq, d),
            to_rows(ktp), to_rows(vtp),
            ks.reshape(dec_b, dec_seq, *hd), vs.reshape(dec_b, dec_seq, *hd),
            vns.reshape(1, dec_b, dec_seq, -1))
```
